```python
import math
import jax, jax.numpy as jnp
from jax import lax
import numpy as np

D_MODEL = 1024
BATCH = 8
SEQ = 2048
DEPTH = 1

CHUNK = 64
D_CONV_A = D_MODEL
KERNEL_A = 31
D_INNER = 2 * D_MODEL
HEAD_DIM_SSM = 64
N_HEADS_SSM = D_INNER // HEAD_DIM_SSM
N_GROUPS = 4
D_STATE = 128
KERNEL_SSM = 4
D_XBC = D_INNER + 2 * N_GROUPS * D_STATE
D_IN_PROJ = 2 * D_CONV_A + D_INNER + D_XBC + N_HEADS_SSM
D_FF = 2816
KERNEL_FFN = 3
N_BRANCH = 2
N_MOD = 6
EPS = 1e-6
EPS_SSM_NORM = 1e-5

kernel_name = "hybrid_conformer_ssd_convffn_block"


def rmsnorm(x, g, eps=EPS):
    xf = x.astype(jnp.float32)
    y = xf * lax.rsqrt(jnp.mean(xf * xf, axis=-1, keepdims=True) + eps)
    return (y * g.astype(jnp.float32)).astype(x.dtype)


def layernorm(x, g, b, eps=EPS):
    xf = x.astype(jnp.float32)
    mu = jnp.mean(xf, axis=-1, keepdims=True)
    var = jnp.mean(jnp.square(xf - mu), axis=-1, keepdims=True)
    y = (xf - mu) * lax.rsqrt(var + eps)
    return (y * g.astype(jnp.float32) + b.astype(jnp.float32)).astype(x.dtype)


def causal_dwconv(x, w, b):
    k = w.shape[0]
    y = lax.conv_general_dilated(
        x, w[:, None, :].astype(x.dtype), window_strides=(1,), padding=[(k - 1, 0)],
        dimension_numbers=("NWC", "WIO", "NWC"), feature_group_count=x.shape[-1])
    return y + b


def modulate(h, shift, scale):
    return h * (1.0 + scale[:, None, :]) + shift[:, None, :]


def ssd_scan(xh, dt, a, bm, cm):
    bsz, seqlen, n_heads, hdim = xh.shape
    nc = seqlen // CHUNK
    hg = n_heads // N_GROUPS
    x = xh.astype(jnp.float32).reshape(bsz, nc, CHUNK, N_GROUPS, hg, hdim)
    dtc = dt.reshape(bsz, nc, CHUNK, N_GROUPS, hg)
    bc = bm.astype(jnp.float32).reshape(bsz, nc, CHUNK, N_GROUPS, D_STATE)
    cc = cm.astype(jnp.float32).reshape(bsz, nc, CHUNK, N_GROUPS, D_STATE)
    da = dtc * a.reshape(N_GROUPS, hg)
    acum = jnp.cumsum(da, axis=2)
    xdt = x * dtc[..., None]
    seg = acum[:, :, :, None] - acum[:, :, None, :]
    causal = jnp.tril(jnp.ones((CHUNK, CHUNK), dtype=bool))[None, None, :, :, None, None]
    decay = jnp.exp(jnp.where(causal, seg, -jnp.inf))
    cb = jnp.einsum("bclgn,bcsgn->bclsg", cc, bc)
    y_diag = jnp.einsum("bclsg,bclsgh,bcsghp->bclghp", cb, decay, xdt)
    decay_to_end = jnp.exp(acum[:, :, -1:] - acum)
    states = jnp.einsum("bclgn,bclgh,bclghp->bcghpn", bc, decay_to_end, xdt)
    chunk_decay = jnp.exp(acum[:, :, -1])

    def step(carry, inp):
        st, dec = inp
        new = carry * dec[..., None, None] + st
        return new, carry

    init = jnp.zeros((bsz, N_GROUPS, hg, hdim, D_STATE), jnp.float32)
    _, prev = lax.scan(step, init, (jnp.swapaxes(states, 0, 1), jnp.swapaxes(chunk_decay, 0, 1)))
    prev = jnp.swapaxes(prev, 0, 1)
    y_off = jnp.einsum("bclgn,bcghpn,bclgh->bclghp", cc, prev, jnp.exp(acum))
    return (y_diag + y_off).reshape(bsz, seqlen, n_heads, hdim)


def setup_inputs(seed: int = 0) -> dict:
    key = jax.random.key(seed)
    ks = jax.random.split(key, 32)

    def nrm(k, shape, scale):
        return jax.random.normal(k, shape, jnp.float32) * scale

    L = DEPTH
    dt_u = jax.random.uniform(ks[13], (L, N_HEADS_SSM), jnp.float32)
    dt0 = jnp.exp(dt_u * (math.log(0.1) - math.log(0.001)) + math.log(0.001))
    dt_bias = dt0 + jnp.log(-jnp.expm1(-dt0))
    return {
        "x": nrm(ks[0], (BATCH, SEQ, D_MODEL), 1.0),
        "c": nrm(ks[1], (BATCH, D_MODEL), 1.0),
        "w_ada": nrm(ks[2], (L, D_MODEL, N_MOD * D_MODEL), 0.5 * D_MODEL ** -0.5),
        "b_ada": nrm(ks[3], (L, N_MOD * D_MODEL), 0.02),
        "norm1_g": 1.0 + nrm(ks[4], (L, D_MODEL), 0.02),
        "w_in": nrm(ks[5], (L, D_MODEL, D_IN_PROJ), D_MODEL ** -0.5),
        "conv_a_w": nrm(ks[6], (L, KERNEL_A, D_CONV_A), KERNEL_A ** -0.5),
        "conv_a_b": nrm(ks[7], (L, D_CONV_A), 0.02),
        "ln_a_g": 1.0 + nrm(ks[8], (L, D_CONV_A), 0.02),
        "ln_a_b": nrm(ks[9], (L, D_CONV_A), 0.02),
        "w_a_out": nrm(ks[10], (L, D_CONV_A, D_MODEL), D_CONV_A ** -0.5),
        "b_a_out": nrm(ks[11], (L, D_MODEL), 0.02),
        "conv_ssm_w": nrm(ks[12], (L, KERNEL_SSM, D_XBC), KERNEL_SSM ** -0.5),
        "conv_ssm_b": nrm(ks[14], (L, D_XBC), 0.02),
        "dt_bias": dt_bias,
        "a_log": jnp.log(jax.random.uniform(ks[15], (L, N_HEADS_SSM), jnp.float32, 1.0, 16.0)),
        "d_skip": 1.0 + nrm(ks[16], (L, N_HEADS_SSM), 0.1),
        "ssm_norm_g": 1.0 + nrm(ks[17], (L, D_INNER), 0.02),
        "w_b_out": nrm(ks[18], (L, D_INNER, D_MODEL), D_INNER ** -0.5),
        "w_gate": nrm(ks[19], (L, D_MODEL, N_BRANCH * D_MODEL), D_MODEL ** -0.5),
        "b_gate": nrm(ks[20], (L, N_BRANCH * D_MODEL), 0.02),
        "w_o": nrm(ks[21], (L, D_MODEL, D_MODEL), D_MODEL ** -0.5),
        "norm2_g": 1.0 + nrm(ks[22], (L, D_MODEL), 0.02),
        "w_up": nrm(ks[23], (L, D_MODEL, 2 * D_FF), D_MODEL ** -0.5),
        "conv_ffn_w": nrm(ks[24], (L, KERNEL_FFN, 2 * D_FF), KERNEL_FFN ** -0.5),
        "conv_ffn_b": nrm(ks[25], (L, 2 * D_FF), 0.02),
        "w_down": nrm(ks[26], (L, D_FF, D_MODEL), D_FF ** -0.5),
        "norm_f_g": 1.0 + nrm(ks[27], (D_MODEL,), 0.02),
    }


def reference(x, c, w_ada, b_ada, norm1_g, w_in, conv_a_w, conv_a_b, ln_a_g, ln_a_b,
              w_a_out, b_a_out, conv_ssm_w, conv_ssm_b, dt_bias, a_log, d_skip,
              ssm_norm_g, w_b_out, w_gate, b_gate, w_o, norm2_g, w_up, conv_ffn_w,
              conv_ffn_b, w_down, norm_f_g):
    bsz, seqlen, _ = x.shape
    for i in range(DEPTH):
        mod = jax.nn.silu(c) @ w_ada[i] + b_ada[i]
        sh1, sc1, gt1, sh2, sc2, gt2 = jnp.split(mod, N_MOD, axis=-1)

        h = modulate(rmsnorm(x, norm1_g[i]), sh1, sc1)
        proj = h @ w_in[i]
        a_val, a_gate, z, xbc, dt_raw = jnp.split(
            proj, np.cumsum([D_CONV_A, D_CONV_A, D_INNER, D_XBC]).tolist(), axis=-1)

        u = a_val * jax.nn.sigmoid(a_gate)
        u = causal_dwconv(u, conv_a_w[i], conv_a_b[i])
        u = jax.nn.silu(layernorm(u, ln_a_g[i], ln_a_b[i]))
        out_a = u @ w_a_out[i] + b_a_out[i]

        xbc = jax.nn.silu(causal_dwconv(xbc, conv_ssm_w[i], conv_ssm_b[i]))
        xs, bm, cm = jnp.split(xbc, [D_INNER, D_INNER + N_GROUPS * D_STATE], axis=-1)
        xs_h = xs.reshape(bsz, seqlen, N_HEADS_SSM, HEAD_DIM_SSM)
        dt = jax.nn.softplus(dt_raw.astype(jnp.float32) + dt_bias[i].astype(jnp.float32))
        a = -jnp.exp(a_log[i].astype(jnp.float32))
        y = ssd_scan(xs_h, dt,  a,
                     bm.reshape(bsz, seqlen, N_GROUPS, D_STATE),
                     cm.reshape(bsz, seqlen, N_GROUPS, D_STATE))
        y = y + xs_h.astype(jnp.float32) * d_skip[i].astype(jnp.float32)[:, None]
        y = y.reshape(bsz, seqlen, D_INNER) * jax.nn.silu(z.astype(jnp.float32))
        yg = y.reshape(bsz, seqlen, N_GROUPS, D_INNER // N_GROUPS)
        yg = yg * lax.rsqrt(jnp.mean(yg * yg, axis=-1, keepdims=True) + EPS_SSM_NORM)
        y = (yg.reshape(bsz, seqlen, D_INNER) * ssm_norm_g[i].astype(jnp.float32)).astype(x.dtype)
        out_b = y @ w_b_out[i]

        g_a, g_b = jnp.split(jax.nn.sigmoid(h @ w_gate[i] + b_gate[i]), N_BRANCH, axis=-1)
        mix = (g_a * out_a + g_b * out_b) @ w_o[i]
        x = x + gt1[:, None, :] * mix

        h2 = modulate(rmsnorm(x, norm2_g[i]), sh2, sc2)
        up = causal_dwconv(h2 @ w_up[i], conv_ffn_w[i], conv_ffn_b[i])
        f_gate, f_val = jnp.split(up, 2, axis=-1)
        x = x + gt2[:, None, :] * ((jax.nn.silu(f_gate) * f_val) @ w_down[i])

    return rmsnorm(x, norm_f_g)
```

```python
import functools

import jax
import jax.numpy as jnp
from jax import lax
from jax.experimental import pallas as pl
from jax.experimental.pallas import tpu as pltpu

F32 = jnp.float32
BF16 = jnp.bfloat16

D_MODEL = 1024
D_CONV_A = D_MODEL
KERNEL_A = 31
D_INNER = 2 * D_MODEL
HEAD_DIM = 64
N_HEADS = D_INNER // HEAD_DIM
N_GROUPS = 4
HEADS_PER_GROUP = N_HEADS // N_GROUPS
D_STATE = 128
KERNEL_SSM = 4
D_XBC = D_INNER + 2 * N_GROUPS * D_STATE
D_FF = 2816
KERNEL_FFN = 3
N_MOD = 6
EPS = 1e-6
EPS_SSM_NORM = 1e-5

LANES = 128
SUBLANES = 8
DT_PAD = LANES
SSD_CHUNK = 128
GROUP_W = HEADS_PER_GROUP * HEAD_DIM

C_AVAL = 0
C_AGATE = C_AVAL + D_CONV_A
C_Z = C_AGATE + D_CONV_A
C_XBC = C_Z + D_INNER
C_DT = C_XBC + D_XBC
C_GATE = C_DT + DT_PAD
C_END = C_GATE + 2 * D_MODEL

VMEM_LIMIT = 56 * 1024 * 1024


def _dot(a, b):
    return jnp.dot(a, b, preferred_element_type=F32)


def _sigmoid(x):
    return jax.nn.sigmoid(x)


def _silu(x):
    return x * _sigmoid(x)


def _resident(shape):
    nd = len(shape)
    return pl.BlockSpec(shape, lambda *_: (0,) * nd, pipeline_mode=pl.Buffered(1))


def _ada_kernel(c_ref, w_ref, b_ref, o_ref):
    s = _silu(c_ref[...]).astype(BF16)
    o_ref[...] = _dot(s, w_ref[...].astype(BF16)) + b_ref[...]


def _ada(c, w_ada, b_ada):
    bsz = c.shape[0]
    n = w_ada.shape[1]
    tn = D_MODEL
    return pl.pallas_call(
        _ada_kernel,
        grid=(n // tn,),
        in_specs=[
            pl.BlockSpec((bsz, D_MODEL), lambda j: (0, 0)),
            pl.BlockSpec((D_MODEL, tn), lambda j: (0, j)),
            pl.BlockSpec((1, tn), lambda j: (0, j)),
        ],
        out_specs=pl.BlockSpec((bsz, tn), lambda j: (0, j)),
        out_shape=jax.ShapeDtypeStruct((bsz, n), F32),
        compiler_params=pltpu.CompilerParams(dimension_semantics=("arbitrary",)),
        name="ada",
    )(c, w_ada, b_ada.reshape(1, n))


def _inproj_kernel(x_ref, mod_ref, n1g_ref, w_ref, bg_ref, dtb_ref,
                   u_ref, zs_ref, xbc_ref, dt_ref, g_ref):
    x = x_ref[...]
    ms = jnp.mean(x * x, axis=-1, keepdims=True)
    y = x * lax.rsqrt(ms + EPS) * n1g_ref[...]
    h = (y * (1.0 + mod_ref[0, 1:2, :]) + mod_ref[0, 0:1, :]).astype(BF16)

    a_val = _dot(h, w_ref[:, C_AVAL:C_AGATE])
    a_gate = _dot(h, w_ref[:, C_AGATE:C_Z])
    u_ref[...] = a_val * _sigmoid(a_gate)

    z = _dot(h, w_ref[:, C_Z:C_XBC])
    zs_ref[...] = _silu(z).astype(BF16)

    xbc_ref[...] = _dot(h, w_ref[:, C_XBC:C_DT])

    dt_raw = _dot(h, w_ref[:, C_DT:C_GATE]) + dtb_ref[...]
    dt_ref[...] = jnp.maximum(dt_raw, 0.0) + jnp.log1p(jnp.exp(-jnp.abs(dt_raw)))

    g = _dot(h, w_ref[:, C_GATE:C_END]) + bg_ref[...]
    g_ref[...] = _sigmoid(g).astype(BF16)


def _inproj(x2d, mod3, n1g, w_cat, b_gate, dt_bias_pad, seqlen, tm):
    t = x2d.shape[0]
    per_seq = seqlen // tm
    row = lambda i: (i, 0)
    return pl.pallas_call(
        _inproj_kernel,
        grid=(t // tm,),
        in_specs=[
            pl.BlockSpec((tm, D_MODEL), row),
            pl.BlockSpec((1, N_MOD, D_MODEL), lambda i: (i // per_seq, 0, 0)),
            _resident((1, D_MODEL)),
            _resident((D_MODEL, C_END)),
            _resident((1, 2 * D_MODEL)),
            _resident((1, DT_PAD)),
        ],
        out_specs=[
            pl.BlockSpec((tm, D_CONV_A), row),
            pl.BlockSpec((tm, D_INNER), row),
            pl.BlockSpec((tm, D_XBC), row),
            pl.BlockSpec((tm, DT_PAD), row),
            pl.BlockSpec((tm, 2 * D_MODEL), row),
        ],
        out_shape=[
            jax.ShapeDtypeStruct((t, D_CONV_A), F32),
            jax.ShapeDtypeStruct((t, D_INNER), BF16),
            jax.ShapeDtypeStruct((t, D_XBC), F32),
            jax.ShapeDtypeStruct((t, DT_PAD), F32),
            jax.ShapeDtypeStruct((t, 2 * D_MODEL), BF16),
        ],
        compiler_params=pltpu.CompilerParams(
            dimension_semantics=("arbitrary",), vmem_limit_bytes=VMEM_LIMIT),
        name="inproj",
    )(x2d, mod3, n1g, w_cat, b_gate, dt_bias_pad)


def _carry_history(ext_ref, halo, tl, first_tile):
    @pl.when(first_tile)
    def _():
        ext_ref[0:halo, :] = jnp.zeros((halo, ext_ref.shape[1]), ext_ref.dtype)

    @pl.when(jnp.logical_not(first_tile))
    def _():
        ext_ref[0:halo, :] = ext_ref[tl:tl + halo, :]


HALO_A = 32
ROWS_A = 32


def _mixa_kernel(u_ref, ga_ref, cw_ref, cb_ref, lg_ref, lb_ref, w_ref, b_ref,
                 o_ref, ext_ref, conv_ref, *, tl):
    _carry_history(ext_ref, HALO_A, tl, pl.program_id(1) == 0)
    ext_ref[HALO_A:HALO_A + tl, :] = u_ref[...]

    base = HALO_A - (KERNEL_A - 1)
    for r in range(tl // ROWS_A):
        r0 = r * ROWS_A
        acc = jnp.broadcast_to(cb_ref[...], (ROWS_A, D_CONV_A))
        for k in range(KERNEL_A):
            acc = acc + cw_ref[k:k + 1, :] * ext_ref[r0 + base + k:r0 + base + k + ROWS_A, :]
        conv_ref[r0:r0 + ROWS_A, :] = acc

    v = conv_ref[...]
    mu = jnp.mean(v, axis=-1, keepdims=True)
    vc = v - mu
    var = jnp.mean(vc * vc, axis=-1, keepdims=True)
    yn = vc * lax.rsqrt(var + EPS) * lg_ref[...] + lb_ref[...]
    s = _silu(yn).astype(BF16)
    out_a = _dot(s, w_ref[...]) + b_ref[...]
    o_ref[...] = ga_ref[...].astype(F32) * out_a


def _mixa(u, g, conv_w, conv_b, ln_g, ln_b, w_out, b_out, bsz, seqlen, tl):
    per_seq = seqlen // tl
    row = lambda b, j: (b * per_seq + j, 0)
    return pl.pallas_call(
        functools.partial(_mixa_kernel, tl=tl),
        grid=(bsz, per_seq),
        in_specs=[
            pl.BlockSpec((tl, D_CONV_A), row),
            pl.BlockSpec((tl, D_MODEL), row),
            _resident((KERNEL_A, D_CONV_A)),
            _resident((1, D_CONV_A)),
            _resident((1, D_CONV_A)),
            _resident((1, D_CONV_A)),
            _resident((D_CONV_A, D_MODEL)),
            _resident((1, D_MODEL)),
        ],
        out_specs=pl.BlockSpec((tl, D_MODEL), row),
        out_shape=jax.ShapeDtypeStruct((bsz * seqlen, D_MODEL), F32),
        scratch_shapes=[
            pltpu.VMEM((HALO_A + tl, D_CONV_A), F32),
            pltpu.VMEM((tl, D_CONV_A), F32),
        ],
        compiler_params=pltpu.CompilerParams(
            dimension_semantics=("arbitrary", "arbitrary"), vmem_limit_bytes=VMEM_LIMIT),
        name="mixa",
    )(u, g, conv_w, conv_b, ln_g, ln_b, w_out, b_out)


HALO_B = SUBLANES
ROWS_B = 32


def _mixb_kernel(xbc_ref, dt_ref, zs_ref, gb_ref, oa_ref, x_ref, mod_ref,
                 cw_ref, cb_ref, alog_ref, dsk_ref, ng_ref, e_ref, wb_ref, wo_ref,
                 o_ref, ext_ref, xc_ref, y_ref, s_ref, *, tl):
    first = pl.program_id(1) == 0
    _carry_history(ext_ref, HALO_B, tl, first)

    @pl.when(first)
    def _():
        s_ref[...] = jnp.zeros(s_ref.shape, s_ref.dtype)

    ext_ref[HALO_B:HALO_B + tl, :] = xbc_ref[...]

    base = HALO_B - (KERNEL_SSM - 1)
    for r in range(tl // ROWS_B):
        r0 = r * ROWS_B
        acc = jnp.broadcast_to(cb_ref[...], (ROWS_B, D_XBC))
        for k in range(KERNEL_SSM):
            acc = acc + cw_ref[k:k + 1, :] * ext_ref[r0 + base + k:r0 + base + k + ROWS_B, :]
        xc_ref[r0:r0 + ROWS_B, :] = _silu(acc)

    q = SSD_CHUNK
    a_neg = -jnp.exp(alog_ref[...])
    rows = lax.broadcasted_iota(jnp.int32, (q, q), 0)
    cols = lax.broadcasted_iota(jnp.int32, (q, q), 1)
    causal = rows >= cols
    tril_f = causal.astype(F32)
    triu_f = (rows <= cols).astype(F32)
    lane = lax.broadcasted_iota(jnp.int32, (q, LANES), 1)
    left = lane < HEAD_DIM
    expand = e_ref[...]

    for ci in range(tl // q):
        r0 = ci * q
        dtq = dt_ref[r0:r0 + q, :]
        da = dtq * a_neg
        acum = jnp.dot(tril_f, da, precision=lax.Precision.HIGHEST,
                       preferred_element_type=F32)
        acum_t = lax.dot_general(da, triu_f, (((0,), (0,)), ((), ())),
                                 precision=lax.Precision.HIGHEST,
                                 preferred_element_type=F32)
        dt_t = dtq.T
        total = acum[q - 1:q, :]
        decay_in = jnp.exp(acum)
        w_state = jnp.exp(total - acum) * dtq
        w_state_x = _dot(w_state.astype(BF16), expand)
        decay_in_x = _dot(decay_in.astype(BF16), expand)

        xs = xc_ref[r0:r0 + q, 0:D_INNER]
        xs_b = xs.astype(BF16)
        xw = (xs * w_state_x).astype(BF16)
        bm = xc_ref[r0:r0 + q, D_INNER:D_INNER + N_GROUPS * D_STATE].astype(BF16)
        cm = xc_ref[r0:r0 + q, D_INNER + N_GROUPS * D_STATE:D_XBC].astype(BF16)

        for g in range(N_GROUPS):
            bg = bm[:, g * D_STATE:(g + 1) * D_STATE]
            cg = cm[:, g * D_STATE:(g + 1) * D_STATE]
            c0 = g * GROUP_W
            cb_mat = lax.dot_general(cg, bg, (((1,), (1,)), ((), ())),
                                     preferred_element_type=F32)
            state = s_ref[:, c0:c0 + GROUP_W]
            y_off = _dot(cg, state.astype(BF16)) * decay_in_x[:, c0:c0 + GROUP_W]

            for pr in range(HEADS_PER_GROUP // 2):
                h1 = g * HEADS_PER_GROUP + 2 * pr
                lc = h1 * HEAD_DIM

                def scores(h):
                    seg = acum[:, h:h + 1] - acum_t[h:h + 1, :]
                    decay = jnp.where(causal, jnp.exp(seg), 0.0)
                    return (decay * cb_mat * dt_t[h:h + 1, :]).astype(BF16)

                lhs = jnp.concatenate([scores(h1), scores(h1 + 1)], axis=1)
                xp = xs_b[:, lc:lc + LANES]
                zero = jnp.zeros_like(xp)
                rhs = jnp.concatenate([jnp.where(left, xp, zero),
                                       jnp.where(left, zero, xp)], axis=0)
                y_ref[r0:r0 + q, lc:lc + LANES] = (
                    _dot(lhs, rhs) + y_off[:, 2 * pr * HEAD_DIM:2 * pr * HEAD_DIM + LANES])

            new_state = lax.dot_general(bg, xw[:, c0:c0 + GROUP_W], (((0,), (0,)), ((), ())),
                                        preferred_element_type=F32)
            s_ref[:, c0:c0 + GROUP_W] = state * decay_in_x[q - 1:q, c0:c0 + GROUP_W] + new_state

    y = y_ref[...] + xc_ref[:, 0:D_INNER] * dsk_ref[...]
    y = y * zs_ref[...].astype(F32)
    parts = []
    for g in range(N_GROUPS):
        yg = y[:, g * GROUP_W:(g + 1) * GROUP_W]
        ms = jnp.mean(yg * yg, axis=-1, keepdims=True)
        parts.append(yg * lax.rsqrt(ms + EPS_SSM_NORM))
    yn = (jnp.concatenate(parts, axis=1) * ng_ref[...]).astype(BF16)
    out_b = _dot(yn, wb_ref[...])
    merged = (oa_ref[...] + gb_ref[...].astype(F32) * out_b).astype(BF16)
    mix = _dot(merged, wo_ref[...])
    o_ref[...] = x_ref[...] + mod_ref[0, 2:3, :] * mix


def _mixb(xbc, dt, zs, g, oa, x2d, mod3, conv_w, conv_b, a_log_pad, d_skip_x, norm_g,
          expand, w_b_out, w_o, bsz, seqlen, tl):
    per_seq = seqlen // tl
    row = lambda b, j: (b * per_seq + j, 0)
    return pl.pallas_call(
        functools.partial(_mixb_kernel, tl=tl),
        grid=(bsz, per_seq),
        in_specs=[
            pl.BlockSpec((tl, D_XBC), row),
            pl.BlockSpec((tl, DT_PAD), row),
            pl.BlockSpec((tl, D_INNER), row),
            pl.BlockSpec((tl, D_MODEL), lambda b, j: (b * per_seq + j, 1)),
            pl.BlockSpec((tl, D_MODEL), row),
            pl.BlockSpec((tl, D_MODEL), row),
            pl.BlockSpec((1, N_MOD, D_MODEL), lambda b, j: (b, 0, 0)),
            _resident((KERNEL_SSM, D_XBC)),
            _resident((1, D_XBC)),
            _resident((1, DT_PAD)),
            _resident((1, D_INNER)),
            _resident((1, D_INNER)),
            _resident((DT_PAD, D_INNER)),
            _resident((D_INNER, D_MODEL)),
            _resident((D_MODEL, D_MODEL)),
        ],
        out_specs=pl.BlockSpec((tl, D_MODEL), row),
        out_shape=jax.ShapeDtypeStruct((bsz * seqlen, D_MODEL), F32),
        scratch_shapes=[
            pltpu.VMEM((HALO_B + tl, D_XBC), F32),
            pltpu.VMEM((tl, D_XBC), F32),
            pltpu.VMEM((tl, D_INNER), F32),
            pltpu.VMEM((D_STATE, D_INNER), F32),
        ],
        compiler_params=pltpu.CompilerParams(
            dimension_semantics=("arbitrary", "arbitrary"), vmem_limit_bytes=VMEM_LIMIT),
        name="mixb",
    )(xbc, dt, zs, g, oa, x2d, mod3, conv_w, conv_b, a_log_pad, d_skip_x, norm_g,
      expand, w_b_out, w_o)


HALO_F = SUBLANES
COLS_F = 256


def _ffn_kernel(x_ref, mod_ref, n2g_ref, wup_ref, cw_ref, cb_ref, wdn_ref, nfg_ref,
                o_ref, ext_ref, *, tl):
    _carry_history(ext_ref, HALO_F, tl, pl.program_id(1) == 0)

    x = x_ref[...]
    ms = jnp.mean(x * x, axis=-1, keepdims=True)
    y = x * lax.rsqrt(ms + EPS) * n2g_ref[...]
    h2 = (y * (1.0 + mod_ref[0, 4:5, :]) + mod_ref[0, 3:4, :]).astype(BF16)

    base = HALO_F - (KERNEL_FFN - 1)

    def conv_cols(c0):
        ext_ref[HALO_F:HALO_F + tl, c0:c0 + COLS_F] = _dot(h2, wup_ref[:, c0:c0 + COLS_F])
        acc = jnp.broadcast_to(cb_ref[:, c0:c0 + COLS_F], (tl, COLS_F))
        for k in range(KERNEL_FFN):
            acc = acc + cw_ref[k:k + 1, c0:c0 + COLS_F] * ext_ref[base + k:base + k + tl, c0:c0 + COLS_F]
        return acc

    down = jnp.zeros((tl, D_MODEL), F32)
    for cblk in range(D_FF // COLS_F):
        c0 = cblk * COLS_F
        f_gate = conv_cols(c0)
        f_val = conv_cols(D_FF + c0)
        act = (_silu(f_gate) * f_val).astype(BF16)
        down = down + _dot(act, wdn_ref[c0:c0 + COLS_F, :])

    xo = x + mod_ref[0, 5:6, :] * down
    ms2 = jnp.mean(xo * xo, axis=-1, keepdims=True)
    o_ref[...] = xo * lax.rsqrt(ms2 + EPS) * nfg_ref[...]


def _ffn(x1, mod3, n2g, w_up, conv_w, conv_b, w_down, nfg, bsz, seqlen, tl):
    per_seq = seqlen // tl
    row = lambda b, j: (b * per_seq + j, 0)
    return pl.pallas_call(
        functools.partial(_ffn_kernel, tl=tl),
        grid=(bsz, per_seq),
        in_specs=[
            pl.BlockSpec((tl, D_MODEL), row),
            pl.BlockSpec((1, N_MOD, D_MODEL), lambda b, j: (b, 0, 0)),
            _resident((1, D_MODEL)),
            _resident((D_MODEL, 2 * D_FF)),
            _resident((KERNEL_FFN, 2 * D_FF)),
            _resident((1, 2 * D_FF)),
            _resident((D_FF, D_MODEL)),
            _resident((1, D_MODEL)),
        ],
        out_specs=pl.BlockSpec((tl, D_MODEL), row),
        out_shape=jax.ShapeDtypeStruct((bsz * seqlen, D_MODEL), F32),
        scratch_shapes=[pltpu.VMEM((HALO_F + tl, 2 * D_FF), F32)],
        compiler_params=pltpu.CompilerParams(
            dimension_semantics=("arbitrary", "arbitrary"), vmem_limit_bytes=VMEM_LIMIT),
        name="ffn",
    )(x1, mod3, n2g, w_up, conv_w, conv_b, w_down, nfg)


def kernel(x, c, w_ada, b_ada, norm1_g, w_in, conv_a_w, conv_a_b, ln_a_g, ln_a_b, w_a_out, b_a_out, conv_ssm_w, conv_ssm_b, dt_bias, a_log, d_skip, ssm_norm_g, w_b_out, w_gate, b_gate, w_o, norm2_g, w_up, conv_ffn_w, conv_ffn_b, w_down, norm_f_g):
    bsz, seqlen, d = x.shape
    assert d == D_MODEL and w_ada.shape[0] == 1
    tm = 256
    tl = 256
    assert seqlen % tl == 0 and seqlen % tm == 0 and tl % SSD_CHUNK == 0

    row = lambda v: v.reshape(1, -1)
    n_dt = N_HEADS
    w_cat = jnp.concatenate(
        [w_in[0][:, :C_DT], jnp.pad(w_in[0][:, C_DT:], ((0, 0), (0, DT_PAD - n_dt))), w_gate[0]],
        axis=1).astype(BF16)
    dt_bias_pad = jnp.pad(dt_bias[0], (0, DT_PAD - n_dt)).reshape(1, DT_PAD)
    a_log_pad = jnp.pad(a_log[0], (0, DT_PAD - n_dt)).reshape(1, DT_PAD)
    d_skip_x = jnp.repeat(d_skip[0], HEAD_DIM).reshape(1, D_INNER)
    expand = (lax.broadcasted_iota(jnp.int32, (DT_PAD, D_INNER), 0)
              == lax.broadcasted_iota(jnp.int32, (DT_PAD, D_INNER), 1) // HEAD_DIM).astype(BF16)

    x2d = x.reshape(bsz * seqlen, D_MODEL)
    mod3 = _ada(c, w_ada[0], b_ada[0]).reshape(bsz, N_MOD, D_MODEL)

    u, zs, xbc, dt, g = _inproj(x2d, mod3, row(norm1_g[0]), w_cat, row(b_gate[0]),
                                dt_bias_pad, seqlen, tm)
    oa = _mixa(u, g, conv_a_w[0], row(conv_a_b[0]), row(ln_a_g[0]), row(ln_a_b[0]),
               w_a_out[0].astype(BF16), row(b_a_out[0]), bsz, seqlen, tl)
    x1 = _mixb(xbc, dt, zs, g, oa, x2d, mod3, conv_ssm_w[0], row(conv_ssm_b[0]),
               a_log_pad, d_skip_x, row(ssm_norm_g[0]), expand,
               w_b_out[0].astype(BF16), w_o[0].astype(BF16), bsz, seqlen, tl)
    out = _ffn(x1, mod3, row(norm2_g[0]), w_up[0].astype(BF16), conv_ffn_w[0],
               row(conv_ffn_b[0]), w_down[0].astype(BF16), row(norm_f_g), bsz, seqlen, tl)
    return out.reshape(bsz, seqlen, D_MODEL)
```

```python
import functools

import jax
import jax.numpy as jnp
from jax import lax
from jax.experimental import pallas as pl
from jax.experimental.pallas import tpu as pltpu

F32 = jnp.float32
BF16 = jnp.bfloat16

D_MODEL = 1024
D_CONV_A = D_MODEL
KERNEL_A = 31
D_INNER = 2 * D_MODEL
HEAD_DIM = 64
N_HEADS = D_INNER // HEAD_DIM
N_GROUPS = 4
HEADS_PER_GROUP = N_HEADS // N_GROUPS
D_STATE = 128
KERNEL_SSM = 4
D_XBC = D_INNER + 2 * N_GROUPS * D_STATE
D_FF = 2816
KERNEL_FFN = 3
N_MOD = 6
EPS = 1e-6
EPS_SSM_NORM = 1e-5

LANES = 128
SUBLANES = 8
DT_PAD = LANES
SSD_CHUNK = 128
GROUP_W = HEADS_PER_GROUP * HEAD_DIM

C_AVAL = 0
C_AGATE = C_AVAL + D_CONV_A
C_Z = C_AGATE + D_CONV_A
C_XBC = C_Z + D_INNER
C_DT = C_XBC + D_XBC
C_GATE = C_DT + DT_PAD
C_END = C_GATE + 2 * D_MODEL

VMEM_LIMIT = 56 * 1024 * 1024


def _dot(a, b):
    return jnp.dot(a, b, preferred_element_type=F32)


def _sigmoid(x):
    return jax.nn.sigmoid(x)


def _silu(x):
    return x * _sigmoid(x)


def _resident(shape):
    nd = len(shape)
    return pl.BlockSpec(shape, lambda *_: (0,) * nd, pipeline_mode=pl.Buffered(1))


def _ada_kernel(c_ref, w_ref, b_ref, o_ref):
    s = _silu(c_ref[...]).astype(BF16)
    o_ref[...] = _dot(s, w_ref[...].astype(BF16)) + b_ref[...]


def _ada(c, w_ada, b_ada):
    bsz = c.shape[0]
    n = w_ada.shape[1]
    tn = D_MODEL
    return pl.pallas_call(
        _ada_kernel,
        grid=(n // tn,),
        in_specs=[
            pl.BlockSpec((bsz, D_MODEL), lambda j: (0, 0)),
            pl.BlockSpec((D_MODEL, tn), lambda j: (0, j)),
            pl.BlockSpec((1, tn), lambda j: (0, j)),
        ],
        out_specs=pl.BlockSpec((bsz, tn), lambda j: (0, j)),
        out_shape=jax.ShapeDtypeStruct((bsz, n), F32),
        compiler_params=pltpu.CompilerParams(dimension_semantics=("arbitrary",)),
        name="ada",
    )(c, w_ada, b_ada.reshape(1, n))


def _inproj_kernel(x_ref, mod_ref, n1g_ref, w_ref, bg_ref, dtb_ref,
                   u_ref, zs_ref, xbc_ref, dt_ref, g_ref):
    x = x_ref[...]
    ms = jnp.mean(x * x, axis=-1, keepdims=True)
    y = x * lax.rsqrt(ms + EPS) * n1g_ref[...]
    h = (y * (1.0 + mod_ref[0, 1:2, :]) + mod_ref[0, 0:1, :]).astype(BF16)

    a_val = _dot(h, w_ref[:, C_AVAL:C_AGATE])
    a_gate = _dot(h, w_ref[:, C_AGATE:C_Z])
    u_ref[...] = a_val * _sigmoid(a_gate)

    z = _dot(h, w_ref[:, C_Z:C_XBC])
    zs_ref[...] = _silu(z).astype(BF16)

    xbc_ref[...] = _dot(h, w_ref[:, C_XBC:C_DT])

    dt_raw = _dot(h, w_ref[:, C_DT:C_GATE]) + dtb_ref[...]
    dt_ref[...] = jnp.maximum(dt_raw, 0.0) + jnp.log1p(jnp.exp(-jnp.abs(dt_raw)))

    g = _dot(h, w_ref[:, C_GATE:C_END]) + bg_ref[...]
    g_ref[...] = _sigmoid(g).astype(BF16)


def _inproj(x2d, mod3, n1g, w_cat, b_gate, dt_bias_pad, seqlen, tm):
    t = x2d.shape[0]
    per_seq = seqlen // tm
    row = lambda i: (i, 0)
    return pl.pallas_call(
        _inproj_kernel,
        grid=(t // tm,),
        in_specs=[
            pl.BlockSpec((tm, D_MODEL), row),
            pl.BlockSpec((1, N_MOD, D_MODEL), lambda i: (i // per_seq, 0, 0)),
            _resident((1, D_MODEL)),
            _resident((D_MODEL, C_END)),
            _resident((1, 2 * D_MODEL)),
            _resident((1, DT_PAD)),
        ],
        out_specs=[
            pl.BlockSpec((tm, D_CONV_A), row),
            pl.BlockSpec((tm, D_INNER), row),
            pl.BlockSpec((tm, D_XBC), row),
            pl.BlockSpec((tm, DT_PAD), row),
            pl.BlockSpec((tm, 2 * D_MODEL), row),
        ],
        out_shape=[
            jax.ShapeDtypeStruct((t, D_CONV_A), F32),
            jax.ShapeDtypeStruct((t, D_INNER), BF16),
            jax.ShapeDtypeStruct((t, D_XBC), F32),
            jax.ShapeDtypeStruct((t, DT_PAD), F32),
            jax.ShapeDtypeStruct((t, 2 * D_MODEL), BF16),
        ],
        compiler_params=pltpu.CompilerParams(
            dimension_semantics=("arbitrary",), vmem_limit_bytes=VMEM_LIMIT),
        name="inproj",
    )(x2d, mod3, n1g, w_cat, b_gate, dt_bias_pad)


def _carry_history(ext_ref, halo, tl, first_tile):
    @pl.when(first_tile)
    def _():
        ext_ref[:, 0:halo, :] = jnp.zeros((ext_ref.shape[0], halo, LANES), ext_ref.dtype)

    @pl.when(jnp.logical_not(first_tile))
    def _():
        ext_ref[:, 0:halo, :] = ext_ref[:, tl:tl + halo, :]


def _causal_conv_slab(ext_ref, slab, w_ref, b_ref, width, halo, r0, rows):
    c0 = slab * LANES
    base = r0 + halo - (width - 1)
    acc = jnp.broadcast_to(b_ref[:, c0:c0 + LANES], (rows, LANES))
    for k in range(width):
        acc = acc + w_ref[k:k + 1, c0:c0 + LANES] * ext_ref[slab, base + k:base + k + rows, :]
    return acc


HALO_A = 32
ROWS_A = 64


def _mixa_kernel(u_ref, ga_ref, cw_ref, cb_ref, lg_ref, lb_ref, w_ref, b_ref,
                 o_ref, ext_ref, conv_ref, *, tl):
    _carry_history(ext_ref, HALO_A, tl, pl.program_id(1) == 0)
    for sl in range(D_CONV_A // LANES):
        ext_ref[sl, HALO_A:HALO_A + tl, :] = u_ref[:, sl * LANES:(sl + 1) * LANES]

    for sl in range(D_CONV_A // LANES):
        for r0 in range(0, tl, ROWS_A):
            conv_ref[r0:r0 + ROWS_A, sl * LANES:(sl + 1) * LANES] = _causal_conv_slab(
                ext_ref, sl, cw_ref, cb_ref, KERNEL_A, HALO_A, r0, ROWS_A)

    v = conv_ref[...]
    mu = jnp.mean(v, axis=-1, keepdims=True)
    vc = v - mu
    var = jnp.mean(vc * vc, axis=-1, keepdims=True)
    yn = vc * lax.rsqrt(var + EPS) * lg_ref[...] + lb_ref[...]
    s = _silu(yn).astype(BF16)
    out_a = _dot(s, w_ref[...]) + b_ref[...]
    o_ref[...] = ga_ref[...].astype(F32) * out_a


def _mixa(u, g, conv_w, conv_b, ln_g, ln_b, w_out, b_out, bsz, seqlen, tl):
    per_seq = seqlen // tl
    row = lambda b, j: (b * per_seq + j, 0)
    return pl.pallas_call(
        functools.partial(_mixa_kernel, tl=tl),
        grid=(bsz, per_seq),
        in_specs=[
            pl.BlockSpec((tl, D_CONV_A), row),
            pl.BlockSpec((tl, D_MODEL), row),
            _resident((KERNEL_A, D_CONV_A)),
            _resident((1, D_CONV_A)),
            _resident((1, D_CONV_A)),
            _resident((1, D_CONV_A)),
            _resident((D_CONV_A, D_MODEL)),
            _resident((1, D_MODEL)),
        ],
        out_specs=pl.BlockSpec((tl, D_MODEL), row),
        out_shape=jax.ShapeDtypeStruct((bsz * seqlen, D_MODEL), F32),
        scratch_shapes=[
            pltpu.VMEM((D_CONV_A // LANES, HALO_A + tl, LANES), F32),
            pltpu.VMEM((tl, D_CONV_A), F32),
        ],
        compiler_params=pltpu.CompilerParams(
            dimension_semantics=("arbitrary", "arbitrary"), vmem_limit_bytes=VMEM_LIMIT),
        name="mixa",
    )(u, g, conv_w, conv_b, ln_g, ln_b, w_out, b_out)


HALO_B = SUBLANES
ROWS_B = 64


def _mixb_kernel(xbc_ref, dt_ref, zs_ref, gb_ref, oa_ref, x_ref, mod_ref,
                 cw_ref, cb_ref, alog_ref, dsk_ref, ng_ref, e_ref, wb_ref, wo_ref,
                 o_ref, ext_ref, xc_ref, y_ref, s_ref, *, tl):
    first = pl.program_id(1) == 0
    _carry_history(ext_ref, HALO_B, tl, first)

    @pl.when(first)
    def _():
        s_ref[...] = jnp.zeros(s_ref.shape, s_ref.dtype)

    for sl in range(D_XBC // LANES):
        ext_ref[sl, HALO_B:HALO_B + tl, :] = xbc_ref[:, sl * LANES:(sl + 1) * LANES]

    for sl in range(D_XBC // LANES):
        for r0 in range(0, tl, ROWS_B):
            xc_ref[r0:r0 + ROWS_B, sl * LANES:(sl + 1) * LANES] = _silu(_causal_conv_slab(
                ext_ref, sl, cw_ref, cb_ref, KERNEL_SSM, HALO_B, r0, ROWS_B))

    q = SSD_CHUNK
    a_neg = -jnp.exp(alog_ref[...])
    rows = lax.broadcasted_iota(jnp.int32, (q, q), 0)
    cols = lax.broadcasted_iota(jnp.int32, (q, q), 1)
    causal = rows >= cols
    tril_f = causal.astype(F32)
    triu_f = (rows <= cols).astype(F32)
    lane = lax.broadcasted_iota(jnp.int32, (q, LANES), 1)
    left = lane < HEAD_DIM
    expand = e_ref[...]

    for ci in range(tl // q):
        r0 = ci * q
        dtq = dt_ref[r0:r0 + q, :]
        da = dtq * a_neg
        acum = jnp.dot(tril_f, da, precision=lax.Precision.HIGHEST,
                       preferred_element_type=F32)
        acum_t = lax.dot_general(da, triu_f, (((0,), (0,)), ((), ())),
                                 precision=lax.Precision.HIGHEST,
                                 preferred_element_type=F32)
        dt_t = dtq.T
        total = acum[q - 1:q, :]
        decay_in = jnp.exp(acum)
        w_state = jnp.exp(total - acum) * dtq
        w_state_x = _dot(w_state.astype(BF16), expand)
        decay_in_x = _dot(decay_in.astype(BF16), expand)

        xs = xc_ref[r0:r0 + q, 0:D_INNER]
        xs_b = xs.astype(BF16)
        xw = (xs * w_state_x).astype(BF16)
        bm = xc_ref[r0:r0 + q, D_INNER:D_INNER + N_GROUPS * D_STATE].astype(BF16)
        cm = xc_ref[r0:r0 + q, D_INNER + N_GROUPS * D_STATE:D_XBC].astype(BF16)

        for g in range(N_GROUPS):
            bg = bm[:, g * D_STATE:(g + 1) * D_STATE]
            cg = cm[:, g * D_STATE:(g + 1) * D_STATE]
            c0 = g * GROUP_W
            cb_mat = lax.dot_general(cg, bg, (((1,), (1,)), ((), ())),
                                     preferred_element_type=F32)
            state = s_ref[:, c0:c0 + GROUP_W]
            y_off = _dot(cg, state.astype(BF16)) * decay_in_x[:, c0:c0 + GROUP_W]

            for pr in range(HEADS_PER_GROUP // 2):
                h1 = g * HEADS_PER_GROUP + 2 * pr
                lc = h1 * HEAD_DIM

                def scores(h):
                    seg = acum[:, h:h + 1] - acum_t[h:h + 1, :]
                    decay = jnp.where(causal, jnp.exp(seg), 0.0)
                    return (decay * cb_mat * dt_t[h:h + 1, :]).astype(BF16)

                lhs = jnp.concatenate([scores(h1), scores(h1 + 1)], axis=1)
                xp = xs_b[:, lc:lc + LANES]
                zero = jnp.zeros_like(xp)
                rhs = jnp.concatenate([jnp.where(left, xp, zero),
                                       jnp.where(left, zero, xp)], axis=0)
                y_ref[r0:r0 + q, lc:lc + LANES] = (
                    _dot(lhs, rhs) + y_off[:, 2 * pr * HEAD_DIM:2 * pr * HEAD_DIM + LANES])

            new_state = lax.dot_general(bg, xw[:, c0:c0 + GROUP_W], (((0,), (0,)), ((), ())),
                                        preferred_element_type=F32)
            s_ref[:, c0:c0 + GROUP_W] = state * decay_in_x[q - 1:q, c0:c0 + GROUP_W] + new_state

    y = y_ref[...] + xc_ref[:, 0:D_INNER] * dsk_ref[...]
    y = y * zs_ref[...].astype(F32)
    parts = []
    for g in range(N_GROUPS):
        yg = y[:, g * GROUP_W:(g + 1) * GROUP_W]
        ms = jnp.mean(yg * yg, axis=-1, keepdims=True)
        parts.append(yg * lax.rsqrt(ms + EPS_SSM_NORM))
    yn = (jnp.concatenate(parts, axis=1) * ng_ref[...]).astype(BF16)
    out_b = _dot(yn, wb_ref[...])
    merged = (oa_ref[...] + gb_ref[...].astype(F32) * out_b).astype(BF16)
    mix = _dot(merged, wo_ref[...])
    o_ref[...] = x_ref[...] + mod_ref[0, 2:3, :] * mix


def _mixb(xbc, dt, zs, g, oa, x2d, mod3, conv_w, conv_b, a_log_pad, d_skip_x, norm_g,
          expand, w_b_out, w_o, bsz, seqlen, tl):
    per_seq = seqlen // tl
    row = lambda b, j: (b * per_seq + j, 0)
    return pl.pallas_call(
        functools.partial(_mixb_kernel, tl=tl),
        grid=(bsz, per_seq),
        in_specs=[
            pl.BlockSpec((tl, D_XBC), row),
            pl.BlockSpec((tl, DT_PAD), row),
            pl.BlockSpec((tl, D_INNER), row),
            pl.BlockSpec((tl, D_MODEL), lambda b, j: (b * per_seq + j, 1)),
            pl.BlockSpec((tl, D_MODEL), row),
            pl.BlockSpec((tl, D_MODEL), row),
            pl.BlockSpec((1, N_MOD, D_MODEL), lambda b, j: (b, 0, 0)),
            _resident((KERNEL_SSM, D_XBC)),
            _resident((1, D_XBC)),
            _resident((1, DT_PAD)),
            _resident((1, D_INNER)),
            _resident((1, D_INNER)),
            _resident((DT_PAD, D_INNER)),
            _resident((D_INNER, D_MODEL)),
            _resident((D_MODEL, D_MODEL)),
        ],
        out_specs=pl.BlockSpec((tl, D_MODEL), row),
        out_shape=jax.ShapeDtypeStruct((bsz * seqlen, D_MODEL), F32),
        scratch_shapes=[
            pltpu.VMEM((D_XBC // LANES, HALO_B + tl, LANES), F32),
            pltpu.VMEM((tl, D_XBC), F32),
            pltpu.VMEM((tl, D_INNER), F32),
            pltpu.VMEM((D_STATE, D_INNER), F32),
        ],
        compiler_params=pltpu.CompilerParams(
            dimension_semantics=("arbitrary", "arbitrary"), vmem_limit_bytes=VMEM_LIMIT),
        name="mixb",
    )(xbc, dt, zs, g, oa, x2d, mod3, conv_w, conv_b, a_log_pad, d_skip_x, norm_g,
      expand, w_b_out, w_o)


HALO_F = SUBLANES
COLS_F = 256


def _ffn_kernel(x_ref, mod_ref, n2g_ref, wup_ref, cw_ref, cb_ref, wdn_ref, nfg_ref,
                o_ref, ext_ref, *, tl):
    _carry_history(ext_ref, HALO_F, tl, pl.program_id(1) == 0)

    x = x_ref[...]
    ms = jnp.mean(x * x, axis=-1, keepdims=True)
    y = x * lax.rsqrt(ms + EPS) * n2g_ref[...]
    h2 = (y * (1.0 + mod_ref[0, 4:5, :]) + mod_ref[0, 3:4, :]).astype(BF16)

    def conv_cols(c0):
        up = _dot(h2, wup_ref[:, c0:c0 + COLS_F])
        parts = []
        for i in range(COLS_F // LANES):
            sl = c0 // LANES + i
            ext_ref[sl, HALO_F:HALO_F + tl, :] = up[:, i * LANES:(i + 1) * LANES]
            parts.append(_causal_conv_slab(ext_ref, sl, cw_ref, cb_ref, KERNEL_FFN, HALO_F, 0, tl))
        return jnp.concatenate(parts, axis=1)

    down = jnp.zeros((tl, D_MODEL), F32)
    for cblk in range(D_FF // COLS_F):
        c0 = cblk * COLS_F
        f_gate = conv_cols(c0)
        f_val = conv_cols(D_FF + c0)
        act = (_silu(f_gate) * f_val).astype(BF16)
        down = down + _dot(act, wdn_ref[c0:c0 + COLS_F, :])

    xo = x + mod_ref[0, 5:6, :] * down
    ms2 = jnp.mean(xo * xo, axis=-1, keepdims=True)
    o_ref[...] = xo * lax.rsqrt(ms2 + EPS) * nfg_ref[...]


def _ffn(x1, mod3, n2g, w_up, conv_w, conv_b, w_down, nfg, bsz, seqlen, tl):
    per_seq = seqlen // tl
    row = lambda b, j: (b * per_seq + j, 0)
    return pl.pallas_call(
        functools.partial(_ffn_kernel, tl=tl),
        grid=(bsz, per_seq),
        in_specs=[
            pl.BlockSpec((tl, D_MODEL), row),
            pl.BlockSpec((1, N_MOD, D_MODEL), lambda b, j: (b, 0, 0)),
            _resident((1, D_MODEL)),
            _resident((D_MODEL, 2 * D_FF)),
            _resident((KERNEL_FFN, 2 * D_FF)),
            _resident((1, 2 * D_FF)),
            _resident((D_FF, D_MODEL)),
            _resident((1, D_MODEL)),
        ],
        out_specs=pl.BlockSpec((tl, D_MODEL), row),
        out_shape=jax.ShapeDtypeStruct((bsz * seqlen, D_MODEL), F32),
        scratch_shapes=[pltpu.VMEM((2 * D_FF // LANES, HALO_F + tl, LANES), F32)],
        compiler_params=pltpu.CompilerParams(
            dimension_semantics=("arbitrary", "arbitrary"), vmem_limit_bytes=VMEM_LIMIT),
        name="ffn",
    )(x1, mod3, n2g, w_up, conv_w, conv_b, w_down, nfg)


def kernel(x, c, w_ada, b_ada, norm1_g, w_in, conv_a_w, conv_a_b, ln_a_g, ln_a_b, w_a_out, b_a_out, conv_ssm_w, conv_ssm_b, dt_bias, a_log, d_skip, ssm_norm_g, w_b_out, w_gate, b_gate, w_o, norm2_g, w_up, conv_ffn_w, conv_ffn_b, w_down, norm_f_g):
    bsz, seqlen, d = x.shape
    assert d == D_MODEL and w_ada.shape[0] == 1
    tm = 256
    tl = 256
    assert seqlen % tl == 0 and seqlen % tm == 0 and tl % SSD_CHUNK == 0

    row = lambda v: v.reshape(1, -1)
    n_dt = N_HEADS
    w_cat = jnp.concatenate(
        [w_in[0][:, :C_DT], jnp.pad(w_in[0][:, C_DT:], ((0, 0), (0, DT_PAD - n_dt))), w_gate[0]],
        axis=1).astype(BF16)
    dt_bias_pad = jnp.pad(dt_bias[0], (0, DT_PAD - n_dt)).reshape(1, DT_PAD)
    a_log_pad = jnp.pad(a_log[0], (0, DT_PAD - n_dt)).reshape(1, DT_PAD)
    d_skip_x = jnp.repeat(d_skip[0], HEAD_DIM).reshape(1, D_INNER)
    expand = (lax.broadcasted_iota(jnp.int32, (DT_PAD, D_INNER), 0)
              == lax.broadcasted_iota(jnp.int32, (DT_PAD, D_INNER), 1) // HEAD_DIM).astype(BF16)

    x2d = x.reshape(bsz * seqlen, D_MODEL)
    mod3 = _ada(c, w_ada[0], b_ada[0]).reshape(bsz, N_MOD, D_MODEL)

    u, zs, xbc, dt, g = _inproj(x2d, mod3, row(norm1_g[0]), w_cat, row(b_gate[0]),
                                dt_bias_pad, seqlen, tm)
    oa = _mixa(u, g, conv_a_w[0], row(conv_a_b[0]), row(ln_a_g[0]), row(ln_a_b[0]),
               w_a_out[0].astype(BF16), row(b_a_out[0]), bsz, seqlen, tl)
    x1 = _mixb(xbc, dt, zs, g, oa, x2d, mod3, conv_ssm_w[0], row(conv_ssm_b[0]),
               a_log_pad, d_skip_x, row(ssm_norm_g[0]), expand,
               w_b_out[0].astype(BF16), w_o[0].astype(BF16), bsz, seqlen, tl)
    out = _ffn(x1, mod3, row(norm2_g[0]), w_up[0].astype(BF16), conv_ffn_w[0],
               row(conv_ffn_b[0]), w_down[0].astype(BF16), row(norm_f_g), bsz, seqlen, tl)
    return out.reshape(bsz, seqlen, D_MODEL)
```

```python
import functools

import jax
import jax.numpy as jnp
from jax import lax
from jax.experimental import pallas as pl
from jax.experimental.pallas import tpu as pltpu

F32 = jnp.float32
BF16 = jnp.bfloat16

D_MODEL = 1024
D_CONV_A = D_MODEL
KERNEL_A = 31
D_INNER = 2 * D_MODEL
HEAD_DIM = 64
N_HEADS = D_INNER // HEAD_DIM
N_GROUPS = 4
HEADS_PER_GROUP = N_HEADS // N_GROUPS
D_STATE = 128
KERNEL_SSM = 4
D_BC = N_GROUPS * D_STATE
D_XBC = D_INNER + 2 * D_BC
D_FF = 2816
KERNEL_FFN = 3
N_MOD = 6
EPS = 1e-6
EPS_SSM_NORM = 1e-5

LANES = 128
SUBLANES = 8
DT_PAD = LANES
SSD_CHUNK = 128
GROUP_W = HEADS_PER_GROUP * HEAD_DIM

C_AVAL = 0
C_AGATE = C_AVAL + D_CONV_A
C_Z = C_AGATE + D_CONV_A
C_XBC = C_Z + D_INNER
C_DT = C_XBC + D_XBC
C_GATE = C_DT + DT_PAD
C_END = C_GATE + 2 * D_MODEL

VMEM_LIMIT = 56 * 1024 * 1024
SEQ_TILE = 256


def _dot(a, b):
    return jnp.dot(a, b, preferred_element_type=F32)


def _sigmoid(x):
    return jax.nn.sigmoid(x)


def _silu(x):
    return x * _sigmoid(x)


def _resident(shape):
    nd = len(shape)
    return pl.BlockSpec(shape, lambda *_: (0,) * nd, pipeline_mode=pl.Buffered(1))


def _seq_call(body, name, bsz, seqlen, tiled_in, resident_in, outs, scratch):
    tl = SEQ_TILE
    per_seq = seqlen // tl
    in_specs, args = [], []
    for arr, cols, cidx in tiled_in:
        if cols is None:
            in_specs.append(pl.BlockSpec((1,) + arr.shape[1:], lambda b, j: (b, 0, 0)))
        else:
            in_specs.append(pl.BlockSpec((tl, cols), lambda b, j, c=cidx: (b * per_seq + j, c)))
        args.append(arr)
    for arr in resident_in:
        in_specs.append(_resident(arr.shape))
        args.append(arr)
    row = lambda b, j: (b * per_seq + j, 0)
    return pl.pallas_call(
        functools.partial(body, tl=tl),
        grid=(bsz, per_seq),
        in_specs=in_specs,
        out_specs=[pl.BlockSpec((tl, cols), row) for cols, _ in outs],
        out_shape=[jax.ShapeDtypeStruct((bsz * seqlen, cols), dt) for cols, dt in outs],
        scratch_shapes=scratch,
        compiler_params=pltpu.CompilerParams(
            dimension_semantics=("arbitrary", "arbitrary"), vmem_limit_bytes=VMEM_LIMIT),
        name=name,
    )(*args)


def _ada_kernel(c_ref, w_ref, b_ref, o_ref):
    s = _silu(c_ref[...]).astype(BF16)
    o_ref[...] = _dot(s, w_ref[...].astype(BF16)) + b_ref[...]


def _ada(c, w_ada, b_ada):
    bsz = c.shape[0]
    n = w_ada.shape[1]
    tn = D_MODEL
    return pl.pallas_call(
        _ada_kernel,
        grid=(n // tn,),
        in_specs=[
            pl.BlockSpec((bsz, D_MODEL), lambda j: (0, 0)),
            pl.BlockSpec((D_MODEL, tn), lambda j: (0, j)),
            pl.BlockSpec((1, tn), lambda j: (0, j)),
        ],
        out_specs=pl.BlockSpec((bsz, tn), lambda j: (0, j)),
        out_shape=jax.ShapeDtypeStruct((bsz, n), F32),
        compiler_params=pltpu.CompilerParams(dimension_semantics=("arbitrary",)),
        name="ada",
    )(c, w_ada, b_ada.reshape(1, n))


def _carry_history(ext_ref, halo, tl, first_tile):
    @pl.when(first_tile)
    def _():
        ext_ref[:, 0:halo, :] = jnp.zeros((ext_ref.shape[0], halo, LANES), ext_ref.dtype)

    @pl.when(jnp.logical_not(first_tile))
    def _():
        ext_ref[:, 0:halo, :] = ext_ref[:, tl:tl + halo, :]


def _fill_slabs(ext_ref, halo, tl, value, first_slab=0):
    for i in range(value.shape[1] // LANES):
        ext_ref[first_slab + i, halo:halo + tl, :] = value[:, i * LANES:(i + 1) * LANES]


def _causal_conv_slab(ext_ref, slab, w_ref, b_ref, width, halo, r0, rows):
    c0 = slab * LANES
    base = r0 + halo - (width - 1)
    acc = jnp.broadcast_to(b_ref[:, c0:c0 + LANES], (rows, LANES))
    for k in range(width):
        acc = acc + w_ref[k:k + 1, c0:c0 + LANES] * ext_ref[slab, base + k:base + k + rows, :]
    return acc


HALO_A = 32
HALO_B = SUBLANES
CONV_ROWS = 64
MXU_TILE = 256
MXU_TILE_COST = 256


def _emit_interleaved(mxu_tasks, valu_tasks):
    t_m = t_v = 0
    vi = 0
    for mi, (fn, cost, epi, need_v) in enumerate(mxu_tasks):
        while vi < need_v:
            valu_tasks[vi][0]()
            t_v += valu_tasks[vi][1]
            vi += 1
        t_v = max(t_v, t_m)
        fn()
        t_m += cost
        t_v += epi
        while (vi < len(valu_tasks) and valu_tasks[vi][2] <= mi + 1
               and t_v + valu_tasks[vi][1] // 2 <= t_m):
            valu_tasks[vi][0]()
            t_v += valu_tasks[vi][1]
            vi += 1
    for fn, _, _ in valu_tasks[vi:]:
        fn()


def _front_kernel(x_ref, mod_ref, n1g_ref, w_ref, bg_ref, dtb_ref,
                  cwa_ref, cba_ref, lg_ref, lb_ref, wa_ref, ba_ref, cws_ref, cbs_ref,
                  oa_ref, zs_ref, xc_ref, dt_ref, gb_ref,
                  exta_ref, extb_ref, conv_ref, s_ref, *, tl):
    first = pl.program_id(1) == 0
    _carry_history(exta_ref, HALO_A, tl, first)
    _carry_history(extb_ref, HALO_B, tl, first)

    x = x_ref[...]
    ms = jnp.mean(x * x, axis=-1, keepdims=True)
    y = x * lax.rsqrt(ms + EPS) * n1g_ref[...]
    h = (y * (1.0 + mod_ref[0, 1:2, :]) + mod_ref[0, 0:1, :]).astype(BF16)

    tn = MXU_TILE
    mxu, valu = [], []

    def proj(c0, n=tn):
        return _dot(h, w_ref[:, c0:c0 + n])

    def glu_task(p):
        def fn():
            a_val = proj(C_AVAL + p * tn)
            a_gate = proj(C_AGATE + p * tn)
            _fill_slabs(exta_ref, HALO_A, tl, a_val * _sigmoid(a_gate), p * tn // LANES)
        return fn

    def xbc_task(t):
        def fn():
            _fill_slabs(extb_ref, HALO_B, tl, proj(C_XBC + t * tn), t * tn // LANES)
        return fn

    def outa_task(t):
        def fn():
            oa_ref[:, t * tn:(t + 1) * tn] = (
                _dot(s_ref[...], wa_ref[:, t * tn:(t + 1) * tn]) + ba_ref[:, t * tn:(t + 1) * tn])
        return fn

    def z_task(t):
        def fn():
            zs_ref[:, t * tn:(t + 1) * tn] = _silu(proj(C_Z + t * tn)).astype(BF16)
        return fn

    def dt_task():
        dt_raw = proj(C_DT, DT_PAD) + dtb_ref[...]
        dt_ref[...] = jnp.maximum(dt_raw, 0.0) + jnp.log1p(jnp.exp(-jnp.abs(dt_raw)))

    def gate_task(t):
        def fn():
            g = _sigmoid(proj(C_GATE + t * tn) + bg_ref[:, t * tn:(t + 1) * tn])
            if t * tn < D_MODEL:
                oa_ref[:, t * tn:(t + 1) * tn] = g * oa_ref[:, t * tn:(t + 1) * tn]
            else:
                gb_ref[:, t * tn - D_MODEL:(t + 1) * tn - D_MODEL] = g.astype(BF16)
        return fn

    def conv_a_task(sl):
        def fn():
            for r0 in range(0, tl, CONV_ROWS):
                conv_ref[r0:r0 + CONV_ROWS, sl * LANES:(sl + 1) * LANES] = _causal_conv_slab(
                    exta_ref, sl, cwa_ref, cba_ref, KERNEL_A, HALO_A, r0, CONV_ROWS)
        return fn

    def ln_task(r0):
        def fn():
            v = conv_ref[r0:r0 + CONV_ROWS, :]
            mu = jnp.mean(v, axis=-1, keepdims=True)
            vc = v - mu
            var = jnp.mean(vc * vc, axis=-1, keepdims=True)
            s_ref[r0:r0 + CONV_ROWS, :] = _silu(
                vc * lax.rsqrt(var + EPS) * lg_ref[...] + lb_ref[...]).astype(BF16)
        return fn

    def conv_b_task(sl):
        def fn():
            for r0 in range(0, tl, CONV_ROWS):
                xc_ref[r0:r0 + CONV_ROWS, sl * LANES:(sl + 1) * LANES] = _silu(_causal_conv_slab(
                    extb_ref, sl, cws_ref, cbs_ref, KERNEL_SSM, HALO_B, r0, CONV_ROWS)).astype(BF16)
        return fn

    n_glu = D_CONV_A // tn
    n_xbc = D_XBC // tn
    for p in range(n_glu):
        mxu.append((glu_task(p), 2 * MXU_TILE_COST, 80, 0))
        for sl in range(p * tn // LANES, (p + 1) * tn // LANES):
            valu.append((conv_a_task(sl), 500, len(mxu)))
    for r0 in range(0, tl, CONV_ROWS):
        valu.append((ln_task(r0), 200, len(mxu)))
    n_ln_done = len(valu)
    for t in range(n_xbc):
        mxu.append((xbc_task(t), MXU_TILE_COST, 0, 0))
        for sl in range(t * tn // LANES, (t + 1) * tn // LANES):
            valu.append((conv_b_task(sl), 115, len(mxu)))
    for t in range(D_INNER // tn):
        mxu.append((z_task(t), MXU_TILE_COST, 80, 0))
    for t in range(D_MODEL // tn):
        mxu.append((outa_task(t), MXU_TILE_COST, 20, n_ln_done))
    mxu.append((dt_task, MXU_TILE_COST // 2, 40, 0))
    for t in range(2 * D_MODEL // tn):
        mxu.append((gate_task(t), MXU_TILE_COST, 80, 0))
    _emit_interleaved(mxu, valu)


def _ssd_kernel(xc_ref, dt_ref, zs_ref, gb_ref, oa_ref, x_ref, mod_ref,
                alog_ref, dsk_ref, ng_ref, e_ref, wb_ref, wo_ref,
                o_ref, y_ref, s_ref, *, tl):
    @pl.when(pl.program_id(1) == 0)
    def _():
        s_ref[...] = jnp.zeros(s_ref.shape, s_ref.dtype)

    q = SSD_CHUNK
    a_neg = -jnp.exp(alog_ref[...])
    rows = lax.broadcasted_iota(jnp.int32, (q, q), 0)
    cols = lax.broadcasted_iota(jnp.int32, (q, q), 1)
    causal = rows >= cols
    tril_b = causal.astype(BF16)
    lane = lax.broadcasted_iota(jnp.int32, (q, LANES), 1)
    left = lane < HEAD_DIM
    expand = e_ref[...]

    for ci in range(tl // q):
        r0 = ci * q
        dtq = dt_ref[r0:r0 + q, :]
        da = dtq * a_neg
        da_hi = da.astype(BF16)
        rem = da - da_hi.astype(F32)
        da_mid = rem.astype(BF16)
        da_lo = (rem - da_mid.astype(F32)).astype(BF16)
        csum = _dot(tril_b, jnp.concatenate([da_hi, da_mid, da_lo], axis=1))
        acum = (csum[:, 0:DT_PAD] + csum[:, DT_PAD:2 * DT_PAD]) + csum[:, 2 * DT_PAD:3 * DT_PAD]
        acum_t = acum.T
        dt_t = dtq.T
        total = acum[q - 1:q, :]
        decay_in = jnp.exp(acum)
        w_state = jnp.exp(total - acum) * dtq
        w_state_x = _dot(w_state.astype(BF16), expand)
        decay_in_x = _dot(decay_in.astype(BF16), expand)

        xs_b = xc_ref[r0:r0 + q, 0:D_INNER]
        xw = (xs_b.astype(F32) * w_state_x).astype(BF16)
        bm = xc_ref[r0:r0 + q, D_INNER:D_INNER + D_BC]
        cm = xc_ref[r0:r0 + q, D_INNER + D_BC:D_XBC]

        for g in range(N_GROUPS):
            bg = bm[:, g * D_STATE:(g + 1) * D_STATE]
            cg = cm[:, g * D_STATE:(g + 1) * D_STATE]
            c0 = g * GROUP_W
            cb_mat = lax.dot_general(cg, bg, (((1,), (1,)), ((), ())),
                                     preferred_element_type=F32)
            state = s_ref[:, c0:c0 + GROUP_W]
            y_off = _dot(cg, state.astype(BF16)) * decay_in_x[:, c0:c0 + GROUP_W]

            for pr in range(HEADS_PER_GROUP // 2):
                h1 = g * HEADS_PER_GROUP + 2 * pr
                lc = h1 * HEAD_DIM

                def scores(h):
                    seg = acum[:, h:h + 1] - acum_t[h:h + 1, :]
                    decay = jnp.where(causal, jnp.exp(seg), 0.0)
                    return (decay * cb_mat * dt_t[h:h + 1, :]).astype(BF16)

                lhs = jnp.concatenate([scores(h1), scores(h1 + 1)], axis=1)
                xp = xs_b[:, lc:lc + LANES]
                zero = jnp.zeros_like(xp)
                rhs = jnp.concatenate([jnp.where(left, xp, zero),
                                       jnp.where(left, zero, xp)], axis=0)
                y_ref[r0:r0 + q, lc:lc + LANES] = (
                    _dot(lhs, rhs) + y_off[:, 2 * pr * HEAD_DIM:2 * pr * HEAD_DIM + LANES])

            new_state = lax.dot_general(bg, xw[:, c0:c0 + GROUP_W], (((0,), (0,)), ((), ())),
                                        preferred_element_type=F32)
            s_ref[:, c0:c0 + GROUP_W] = state * decay_in_x[q - 1:q, c0:c0 + GROUP_W] + new_state

    y = y_ref[...] + xc_ref[:, 0:D_INNER].astype(F32) * dsk_ref[...]
    y = y * zs_ref[...].astype(F32)
    parts = []
    for g in range(N_GROUPS):
        yg = y[:, g * GROUP_W:(g + 1) * GROUP_W]
        ms = jnp.mean(yg * yg, axis=-1, keepdims=True)
        parts.append(yg * lax.rsqrt(ms + EPS_SSM_NORM))
    yn = (jnp.concatenate(parts, axis=1) * ng_ref[...]).astype(BF16)
    out_b = _dot(yn, wb_ref[...])
    merged = (oa_ref[...] + gb_ref[...].astype(F32) * out_b).astype(BF16)
    mix = _dot(merged, wo_ref[...])
    o_ref[...] = x_ref[...] + mod_ref[0, 2:3, :] * mix


HALO_F = SUBLANES
COLS_F = 256


def _ffn_kernel(x_ref, mod_ref, n2g_ref, wup_ref, cw_ref, cb_ref, wdn_ref, nfg_ref,
                o_ref, ext_ref, *, tl):
    _carry_history(ext_ref, HALO_F, tl, pl.program_id(1) == 0)

    x = x_ref[...]
    ms = jnp.mean(x * x, axis=-1, keepdims=True)
    y = x * lax.rsqrt(ms + EPS) * n2g_ref[...]
    h2 = (y * (1.0 + mod_ref[0, 4:5, :]) + mod_ref[0, 3:4, :]).astype(BF16)

    def up_cols(cblk):
        for c0 in (cblk * COLS_F, D_FF + cblk * COLS_F):
            _fill_slabs(ext_ref, HALO_F, tl, _dot(h2, wup_ref[:, c0:c0 + COLS_F]), c0 // LANES)

    def act_cols(cblk):
        def conv(c0):
            return jnp.concatenate(
                [_causal_conv_slab(ext_ref, c0 // LANES + i, cw_ref, cb_ref, KERNEL_FFN, HALO_F, 0, tl)
                 for i in range(COLS_F // LANES)], axis=1)
        return (_silu(conv(cblk * COLS_F)) * conv(D_FF + cblk * COLS_F)).astype(BF16)

    nblk = D_FF // COLS_F
    down = jnp.zeros((tl, D_MODEL), F32)
    up_cols(0)
    for cblk in range(nblk):
        if cblk + 1 < nblk:
            up_cols(cblk + 1)
        down = down + _dot(act_cols(cblk), wdn_ref[cblk * COLS_F:(cblk + 1) * COLS_F, :])

    xo = x + mod_ref[0, 5:6, :] * down
    ms2 = jnp.mean(xo * xo, axis=-1, keepdims=True)
    o_ref[...] = xo * lax.rsqrt(ms2 + EPS) * nfg_ref[...]


def kernel(x, c, w_ada, b_ada, norm1_g, w_in, conv_a_w, conv_a_b, ln_a_g, ln_a_b, w_a_out, b_a_out, conv_ssm_w, conv_ssm_b, dt_bias, a_log, d_skip, ssm_norm_g, w_b_out, w_gate, b_gate, w_o, norm2_g, w_up, conv_ffn_w, conv_ffn_b, w_down, norm_f_g):
    bsz, seqlen, d = x.shape
    assert d == D_MODEL and w_ada.shape[0] == 1
    tl = SEQ_TILE
    assert seqlen % tl == 0 and tl % SSD_CHUNK == 0

    row = lambda v: v.reshape(1, -1)
    n_dt = N_HEADS
    w_cat = jnp.concatenate(
        [w_in[0][:, :C_DT], jnp.pad(w_in[0][:, C_DT:], ((0, 0), (0, DT_PAD - n_dt))), w_gate[0]],
        axis=1).astype(BF16)
    dt_bias_pad = jnp.pad(dt_bias[0], (0, DT_PAD - n_dt)).reshape(1, DT_PAD)
    a_log_pad = jnp.pad(a_log[0], (0, DT_PAD - n_dt)).reshape(1, DT_PAD)
    d_skip_x = jnp.repeat(d_skip[0], HEAD_DIM).reshape(1, D_INNER)
    expand = (lax.broadcasted_iota(jnp.int32, (DT_PAD, D_INNER), 0)
              == lax.broadcasted_iota(jnp.int32, (DT_PAD, D_INNER), 1) // HEAD_DIM).astype(BF16)

    x2d = x.reshape(bsz * seqlen, D_MODEL)
    mod3 = _ada(c, w_ada[0], b_ada[0]).reshape(bsz, N_MOD, D_MODEL)

    oa, zs, xc, dt, gb = _seq_call(
        _front_kernel, "front", bsz, seqlen,
        [(x2d, D_MODEL, 0), (mod3, None, None)],
        [row(norm1_g[0]), w_cat, row(b_gate[0]), dt_bias_pad,
         conv_a_w[0], row(conv_a_b[0]), row(ln_a_g[0]), row(ln_a_b[0]),
         w_a_out[0].astype(BF16), row(b_a_out[0]), conv_ssm_w[0], row(conv_ssm_b[0])],
        [(D_MODEL, F32), (D_INNER, BF16), (D_XBC, BF16), (DT_PAD, F32), (D_MODEL, BF16)],
        [pltpu.VMEM((D_CONV_A // LANES, HALO_A + tl, LANES), F32),
         pltpu.VMEM((D_XBC // LANES, HALO_B + tl, LANES), F32),
         pltpu.VMEM((tl, D_CONV_A), F32),
         pltpu.VMEM((tl, D_CONV_A), BF16)])

    (x1,) = _seq_call(
        _ssd_kernel, "ssd", bsz, seqlen,
        [(xc, D_XBC, 0), (dt, DT_PAD, 0), (zs, D_INNER, 0), (gb, D_MODEL, 0), (oa, D_MODEL, 0),
         (x2d, D_MODEL, 0), (mod3, None, None)],
        [a_log_pad, d_skip_x, row(ssm_norm_g[0]), expand,
         w_b_out[0].astype(BF16), w_o[0].astype(BF16)],
        [(D_MODEL, F32)],
        [pltpu.VMEM((tl, D_INNER), F32), pltpu.VMEM((D_STATE, D_INNER), F32)])

    (out,) = _seq_call(
        _ffn_kernel, "ffn", bsz, seqlen,
        [(x1, D_MODEL, 0), (mod3, None, None)],
        [row(norm2_g[0]), w_up[0].astype(BF16), conv_ffn_w[0], row(conv_ffn_b[0]),
         w_down[0].astype(BF16), row(norm_f_g)],
        [(D_MODEL, F32)],
        [pltpu.VMEM((2 * D_FF // LANES, HALO_F + tl, LANES), F32)])
    return out.reshape(bsz, seqlen, D_MODEL)
```

```python
import functools

import jax
import jax.numpy as jnp
from jax import lax
from jax.experimental import pallas as pl
from jax.experimental.pallas import tpu as pltpu

F32 = jnp.float32
BF16 = jnp.bfloat16

D_MODEL = 1024
D_CONV_A = D_MODEL
KERNEL_A = 31
D_INNER = 2 * D_MODEL
HEAD_DIM = 64
N_HEADS = D_INNER // HEAD_DIM
N_GROUPS = 4
HEADS_PER_GROUP = N_HEADS // N_GROUPS
D_STATE = 128
KERNEL_SSM = 4
D_BC = N_GROUPS * D_STATE
D_XBC = D_INNER + 2 * D_BC
D_FF = 2816
KERNEL_FFN = 3
N_MOD = 6
EPS = 1e-6
EPS_SSM_NORM = 1e-5
LOG2_E = 1.4426950408889634

LANES = 128
SUBLANES = 8
DT_PAD = LANES
SSD_CHUNK = 128
GROUP_W = HEADS_PER_GROUP * HEAD_DIM

C_AVAL = 0
C_AGATE = C_AVAL + D_CONV_A
C_Z = C_AGATE + D_CONV_A
C_XBC = C_Z + D_INNER
C_DT = C_XBC + D_XBC

VMEM_LIMIT = 56 * 1024 * 1024
SEQ_TILE = 256


def _dot(a, b):
    return jnp.dot(a, b, preferred_element_type=F32)


def _sigmoid(x):
    return jax.nn.sigmoid(x)


def _silu(x):
    return x * _sigmoid(x)


def _resident(shape):
    nd = len(shape)
    return pl.BlockSpec(shape, lambda *_: (0,) * nd, pipeline_mode=pl.Buffered(1))


def _seq_call(body, name, bsz, seqlen, tiled_in, resident_in, outs, scratch):
    tl = SEQ_TILE
    per_seq = seqlen // tl
    in_specs, args = [], []
    for arr, cols, cidx in tiled_in:
        if cols is None:
            in_specs.append(pl.BlockSpec((1,) + arr.shape[1:], lambda b, j: (b, 0, 0)))
        else:
            in_specs.append(pl.BlockSpec((tl, cols), lambda b, j, c=cidx: (b * per_seq + j, c)))
        args.append(arr)
    for arr in resident_in:
        in_specs.append(_resident(arr.shape))
        args.append(arr)
    row = lambda b, j: (b * per_seq + j, 0)
    return pl.pallas_call(
        functools.partial(body, tl=tl),
        grid=(bsz, per_seq),
        in_specs=in_specs,
        out_specs=[pl.BlockSpec((tl, cols), row) for cols, _ in outs],
        out_shape=[jax.ShapeDtypeStruct((bsz * seqlen, cols), dt) for cols, dt in outs],
        scratch_shapes=scratch,
        compiler_params=pltpu.CompilerParams(
            dimension_semantics=("arbitrary", "arbitrary"), vmem_limit_bytes=VMEM_LIMIT),
        name=name,
    )(*args)


def _ada_kernel(c_ref, w_ref, b_ref, o_ref):
    s = _silu(c_ref[...]).astype(BF16)
    o_ref[...] = _dot(s, w_ref[...].astype(BF16)) + b_ref[...]


def _ada(c, w_ada, b_ada):
    bsz = c.shape[0]
    n = w_ada.shape[1]
    tn = D_MODEL
    return pl.pallas_call(
        _ada_kernel,
        grid=(n // tn,),
        in_specs=[
            pl.BlockSpec((bsz, D_MODEL), lambda j: (0, 0)),
            pl.BlockSpec((D_MODEL, tn), lambda j: (0, j)),
            pl.BlockSpec((1, tn), lambda j: (0, j)),
        ],
        out_specs=pl.BlockSpec((bsz, tn), lambda j: (0, j)),
        out_shape=jax.ShapeDtypeStruct((bsz, n), F32),
        compiler_params=pltpu.CompilerParams(dimension_semantics=("arbitrary",)),
        name="ada",
    )(c, w_ada, b_ada.reshape(1, n))


def _carry_history(ext_ref, halo, tl, first_tile):
    @pl.when(first_tile)
    def _():
        ext_ref[:, 0:halo, :] = jnp.zeros((ext_ref.shape[0], halo, LANES), ext_ref.dtype)

    @pl.when(jnp.logical_not(first_tile))
    def _():
        ext_ref[:, 0:halo, :] = ext_ref[:, tl:tl + halo, :]


def _fill_slabs(ext_ref, halo, tl, value, first_slab=0):
    for i in range(value.shape[1] // LANES):
        ext_ref[first_slab + i, halo:halo + tl, :] = value[:, i * LANES:(i + 1) * LANES]


def _causal_conv_slab(ext_ref, slab, w_ref, b_ref, width, halo, r0, rows):
    c0 = slab * LANES
    base = r0 + halo - (width - 1)
    acc = jnp.broadcast_to(b_ref[:, c0:c0 + LANES], (rows, LANES))
    for k in range(width):
        acc = acc + w_ref[k:k + 1, c0:c0 + LANES] * ext_ref[slab, base + k:base + k + rows, :]
    return acc


HALO_A = 32
HALO_B = SUBLANES
CONV_ROWS = 64
MXU_TILE = 256
MXU_TILE_COST = 256


def _emit_interleaved(mxu_tasks, valu_tasks):
    t_m = t_v = 0
    vi = 0
    for mi, (fn, cost, epi, need_v) in enumerate(mxu_tasks):
        while vi < need_v:
            valu_tasks[vi][0]()
            t_v += valu_tasks[vi][1]
            vi += 1
        t_v = max(t_v, t_m)
        fn()
        t_m += cost
        t_v += epi
        while (vi < len(valu_tasks) and valu_tasks[vi][2] <= mi + 1
               and t_v + valu_tasks[vi][1] // 2 <= t_m):
            valu_tasks[vi][0]()
            t_v += valu_tasks[vi][1]
            vi += 1
    for fn, _, _ in valu_tasks[vi:]:
        fn()


def _front_kernel(x_ref, mod_ref, n1g_ref, w_ref, wg_ref, bg_ref, dtb_ref,
                  cwa_ref, cba_ref, lg_ref, lb_ref, wa_ref, ba_ref, cws_ref, cbs_ref,
                  oa_ref, zs_ref, xc_ref, dt_ref, gb_ref,
                  exta_ref, extb_ref, conv_ref, s_ref, *, tl):
    first = pl.program_id(1) == 0
    _carry_history(exta_ref, HALO_A, tl, first)
    _carry_history(extb_ref, HALO_B, tl, first)

    x = x_ref[...]
    ms = jnp.mean(x * x, axis=-1, keepdims=True)
    y = x * lax.rsqrt(ms + EPS) * n1g_ref[...]
    h = (y * (1.0 + mod_ref[0, 1:2, :]) + mod_ref[0, 0:1, :]).astype(BF16)

    tn = MXU_TILE
    mxu, valu = [], []

    def proj(c0, n=tn):
        return _dot(h, w_ref[:, c0:c0 + n])

    def glu_task(p):
        def fn():
            a_val = proj(C_AVAL + p * tn)
            a_gate = proj(C_AGATE + p * tn)
            _fill_slabs(exta_ref, HALO_A, tl, a_val * _sigmoid(a_gate), p * tn // LANES)
        return fn

    def xbc_task(t):
        def fn():
            _fill_slabs(extb_ref, HALO_B, tl, proj(C_XBC + t * tn), t * tn // LANES)
        return fn

    def outa_task(t):
        def fn():
            oa_ref[:, t * tn:(t + 1) * tn] = (
                _dot(s_ref[...], wa_ref[:, t * tn:(t + 1) * tn]) + ba_ref[:, t * tn:(t + 1) * tn])
        return fn

    def z_task(t):
        def fn():
            zs_ref[:, t * tn:(t + 1) * tn] = _silu(proj(C_Z + t * tn)).astype(BF16)
        return fn

    def dt_task():
        dt_raw = proj(C_DT, DT_PAD) + dtb_ref[...]
        dt_ref[...] = jnp.maximum(dt_raw, 0.0) + jnp.log1p(jnp.exp(-jnp.abs(dt_raw)))

    def gate_task(t):
        def fn():
            g = _sigmoid(_dot(h, wg_ref[:, t * tn:(t + 1) * tn]) + bg_ref[:, t * tn:(t + 1) * tn])
            if t * tn < D_MODEL:
                oa_ref[:, t * tn:(t + 1) * tn] = g * oa_ref[:, t * tn:(t + 1) * tn]
            else:
                gb_ref[:, t * tn - D_MODEL:(t + 1) * tn - D_MODEL] = g.astype(BF16)
        return fn

    def conv_a_task(sl):
        def fn():
            for r0 in range(0, tl, CONV_ROWS):
                conv_ref[r0:r0 + CONV_ROWS, sl * LANES:(sl + 1) * LANES] = _causal_conv_slab(
                    exta_ref, sl, cwa_ref, cba_ref, KERNEL_A, HALO_A, r0, CONV_ROWS)
        return fn

    def ln_task(r0):
        def fn():
            v = conv_ref[r0:r0 + CONV_ROWS, :]
            mu = jnp.mean(v, axis=-1, keepdims=True)
            vc = v - mu
            var = jnp.mean(vc * vc, axis=-1, keepdims=True)
            s_ref[r0:r0 + CONV_ROWS, :] = _silu(
                vc * lax.rsqrt(var + EPS) * lg_ref[...] + lb_ref[...]).astype(BF16)
        return fn

    def conv_b_task(sl):
        def fn():
            for r0 in range(0, tl, CONV_ROWS):
                xc_ref[r0:r0 + CONV_ROWS, sl * LANES:(sl + 1) * LANES] = _silu(_causal_conv_slab(
                    extb_ref, sl, cws_ref, cbs_ref, KERNEL_SSM, HALO_B, r0, CONV_ROWS)).astype(BF16)
        return fn

    n_glu = D_CONV_A // tn
    n_xbc = D_XBC // tn
    for p in range(n_glu):
        mxu.append((glu_task(p), 2 * MXU_TILE_COST, 80, 0))
        for sl in range(p * tn // LANES, (p + 1) * tn // LANES):
            valu.append((conv_a_task(sl), 500, len(mxu)))
    for r0 in range(0, tl, CONV_ROWS):
        valu.append((ln_task(r0), 200, len(mxu)))
    n_ln_done = len(valu)
    for t in range(n_xbc):
        mxu.append((xbc_task(t), MXU_TILE_COST, 0, 0))
        for sl in range(t * tn // LANES, (t + 1) * tn // LANES):
            valu.append((conv_b_task(sl), 115, len(mxu)))
    for t in range(D_INNER // tn):
        mxu.append((z_task(t), MXU_TILE_COST, 80, 0))
    for t in range(D_MODEL // tn):
        mxu.append((outa_task(t), MXU_TILE_COST, 20, n_ln_done))
    mxu.append((dt_task, MXU_TILE_COST // 2, 40, 0))
    for t in range(2 * D_MODEL // tn):
        mxu.append((gate_task(t), MXU_TILE_COST, 80, 0))
    _emit_interleaved(mxu, valu)


def _ssd_kernel(xc_ref, dt_ref, zs_ref, gb_ref, oa_ref, x_ref, mod_ref,
                alog_ref, dsk_ref, ng_ref, e_ref, wb_ref, wo_ref,
                o_ref, y_ref, s_ref, yn_ref, mg_ref, *, tl):
    @pl.when(pl.program_id(1) == 0)
    def _():
        s_ref[...] = jnp.zeros(s_ref.shape, s_ref.dtype)

    q = SSD_CHUNK
    a_neg = -jnp.exp(alog_ref[...])
    rows = lax.broadcasted_iota(jnp.int32, (q, q), 0)
    cols = lax.broadcasted_iota(jnp.int32, (q, q), 1)
    causal = rows >= cols
    tril_b = causal.astype(BF16)
    lane = lax.broadcasted_iota(jnp.int32, (q, LANES), 1)
    left = lane < HEAD_DIM
    expand = e_ref[...]

    def chunk_prep(ci):
        r0 = ci * q
        dtq = dt_ref[r0:r0 + q, :]
        da = dtq * a_neg
        da_hi = da.astype(BF16)
        rem = da - da_hi.astype(F32)
        da_mid = rem.astype(BF16)
        da_lo = (rem - da_mid.astype(F32)).astype(BF16)
        csum = _dot(tril_b, jnp.concatenate([da_hi, da_mid, da_lo], axis=1))
        acum = (csum[:, 0:DT_PAD] + csum[:, DT_PAD:2 * DT_PAD]) + csum[:, 2 * DT_PAD:3 * DT_PAD]
        total = acum[q - 1:q, :]
        w_state = jnp.exp(total - acum) * dtq
        col2 = acum * LOG2_E
        row2_t = (col2 - jnp.log2(dtq)).T
        return dict(
            col2=col2, row2_t=row2_t,
            w_state_x=_dot(w_state.astype(BF16), expand),
            decay_in_x=_dot(jnp.exp(acum).astype(BF16), expand))

    def group_head(ci, g, ck):
        r0 = ci * q
        c0 = g * GROUP_W
        bg = xc_ref[r0:r0 + q, D_INNER + g * D_STATE:D_INNER + (g + 1) * D_STATE]
        cg = xc_ref[r0:r0 + q, D_INNER + D_BC + g * D_STATE:D_INNER + D_BC + (g + 1) * D_STATE]
        cb_mat = lax.dot_general(cg, bg, (((1,), (1,)), ((), ())),
                                 preferred_element_type=F32)
        y_off = _dot(cg, s_ref[:, c0:c0 + GROUP_W].astype(BF16))
        return cb_mat, y_off

    def group_tail(ci, g, ck, cb_mat, y_off):
        r0 = ci * q
        c0 = g * GROUP_W
        col2, row2_t = ck["col2"], ck["row2_t"]
        decay_in_x = ck["decay_in_x"][:, c0:c0 + GROUP_W]
        y_off = y_off * decay_in_x
        for pr in range(HEADS_PER_GROUP // 2):
            h1 = g * HEADS_PER_GROUP + 2 * pr

            def scores(h):
                seg2 = col2[:, h:h + 1] - row2_t[h:h + 1, :]
                return (jnp.where(causal, jnp.exp2(seg2), 0.0) * cb_mat).astype(BF16)

            lhs = jnp.concatenate([scores(h1), scores(h1 + 1)], axis=1)
            xp = xc_ref[r0:r0 + q, c0 + pr * LANES:c0 + (pr + 1) * LANES]
            zero = jnp.zeros_like(xp)
            rhs = jnp.concatenate([jnp.where(left, xp, zero),
                                   jnp.where(left, zero, xp)], axis=0)
            y_ref[r0:r0 + q, c0 + pr * LANES:c0 + (pr + 1) * LANES] = (
                _dot(lhs, rhs) + y_off[:, pr * LANES:(pr + 1) * LANES])

        bg = xc_ref[r0:r0 + q, D_INNER + g * D_STATE:D_INNER + (g + 1) * D_STATE]
        xs_b = xc_ref[r0:r0 + q, c0:c0 + GROUP_W]
        xw = (xs_b.astype(F32) * ck["w_state_x"][:, c0:c0 + GROUP_W]).astype(BF16)
        new_state = lax.dot_general(bg, xw, (((0,), (0,)), ((), ())),
                                    preferred_element_type=F32)
        s_ref[:, c0:c0 + GROUP_W] = s_ref[:, c0:c0 + GROUP_W] * decay_in_x[q - 1:q, :] + new_state

    n_chunks = tl // q
    ck = chunk_prep(0)
    head = group_head(0, 0, ck)
    for ci in range(n_chunks):
        ck_next = None
        for g in range(N_GROUPS):
            cur = head
            if g + 1 < N_GROUPS:
                head = group_head(ci, g + 1, ck)
            elif ci + 1 < n_chunks:
                ck_next = chunk_prep(ci + 1)
            group_tail(ci, g, ck, *cur)
        if ci + 1 < n_chunks:
            ck = ck_next
            head = group_head(ci + 1, 0, ck)

    y = y_ref[...] + xc_ref[:, 0:D_INNER].astype(F32) * dsk_ref[...]
    y = y * zs_ref[...].astype(F32)
    for g in range(N_GROUPS):
        yg = y[:, g * GROUP_W:(g + 1) * GROUP_W]
        ms = jnp.mean(yg * yg, axis=-1, keepdims=True)
        yn_ref[:, g * GROUP_W:(g + 1) * GROUP_W] = (
            yg * lax.rsqrt(ms + EPS_SSM_NORM) * ng_ref[:, g * GROUP_W:(g + 1) * GROUP_W]).astype(BF16)
    out_b = _dot(yn_ref[...], wb_ref[...])
    mg_ref[...] = (oa_ref[...] + gb_ref[...].astype(F32) * out_b).astype(BF16)
    mix = _dot(mg_ref[...], wo_ref[...])
    o_ref[...] = x_ref[...] + mod_ref[0, 2:3, :] * mix


HALO_F = SUBLANES
COLS_F = 256


def _ffn_kernel(x_ref, mod_ref, n2g_ref, wup_ref, cw_ref, cb_ref, wdn_ref, nfg_ref,
                o_ref, ext_ref, *, tl):
    _carry_history(ext_ref, HALO_F, tl, pl.program_id(1) == 0)

    x = x_ref[...]
    ms = jnp.mean(x * x, axis=-1, keepdims=True)
    y = x * lax.rsqrt(ms + EPS) * n2g_ref[...]
    h2 = (y * (1.0 + mod_ref[0, 4:5, :]) + mod_ref[0, 3:4, :]).astype(BF16)

    def up_cols(cblk):
        for c0 in (cblk * COLS_F, D_FF + cblk * COLS_F):
            _fill_slabs(ext_ref, HALO_F, tl, _dot(h2, wup_ref[:, c0:c0 + COLS_F]), c0 // LANES)

    def act_cols(cblk):
        def conv(c0):
            return jnp.concatenate(
                [_causal_conv_slab(ext_ref, c0 // LANES + i, cw_ref, cb_ref, KERNEL_FFN, HALO_F, 0, tl)
                 for i in range(COLS_F // LANES)], axis=1)
        return (_silu(conv(cblk * COLS_F)) * conv(D_FF + cblk * COLS_F)).astype(BF16)

    nblk = D_FF // COLS_F
    down = jnp.zeros((tl, D_MODEL), F32)
    up_cols(0)
    for cblk in range(nblk):
        if cblk + 1 < nblk:
            up_cols(cblk + 1)
        down = down + _dot(act_cols(cblk), wdn_ref[cblk * COLS_F:(cblk + 1) * COLS_F, :])

    xo = x + mod_ref[0, 5:6, :] * down
    ms2 = jnp.mean(xo * xo, axis=-1, keepdims=True)
    o_ref[...] = xo * lax.rsqrt(ms2 + EPS) * nfg_ref[...]


def kernel(x, c, w_ada, b_ada, norm1_g, w_in, conv_a_w, conv_a_b, ln_a_g, ln_a_b, w_a_out, b_a_out, conv_ssm_w, conv_ssm_b, dt_bias, a_log, d_skip, ssm_norm_g, w_b_out, w_gate, b_gate, w_o, norm2_g, w_up, conv_ffn_w, conv_ffn_b, w_down, norm_f_g):
    bsz, seqlen, d = x.shape
    assert d == D_MODEL and w_ada.shape[0] == 1
    tl = SEQ_TILE
    assert seqlen % tl == 0 and tl % SSD_CHUNK == 0

    row = lambda v: v.reshape(1, -1)
    n_dt = N_HEADS
    w_in_p = jnp.pad(w_in[0], ((0, 0), (0, DT_PAD - n_dt))).astype(BF16)
    dt_bias_pad = jnp.pad(dt_bias[0], (0, DT_PAD - n_dt)).reshape(1, DT_PAD)
    a_log_pad = jnp.pad(a_log[0], (0, DT_PAD - n_dt)).reshape(1, DT_PAD)
    d_skip_x = jnp.repeat(d_skip[0], HEAD_DIM).reshape(1, D_INNER)
    expand = (lax.broadcasted_iota(jnp.int32, (DT_PAD, D_INNER), 0)
              == lax.broadcasted_iota(jnp.int32, (DT_PAD, D_INNER), 1) // HEAD_DIM).astype(BF16)

    x2d = x.reshape(bsz * seqlen, D_MODEL)
    mod3 = _ada(c, w_ada[0], b_ada[0]).reshape(bsz, N_MOD, D_MODEL)

    oa, zs, xc, dt, gb = _seq_call(
        _front_kernel, "front", bsz, seqlen,
        [(x2d, D_MODEL, 0), (mod3, None, None)],
        [row(norm1_g[0]), w_in_p, w_gate[0].astype(BF16), row(b_gate[0]), dt_bias_pad,
         conv_a_w[0], row(conv_a_b[0]), row(ln_a_g[0]), row(ln_a_b[0]),
         w_a_out[0].astype(BF16), row(b_a_out[0]), conv_ssm_w[0], row(conv_ssm_b[0])],
        [(D_MODEL, F32), (D_INNER, BF16), (D_XBC, BF16), (DT_PAD, F32), (D_MODEL, BF16)],
        [pltpu.VMEM((D_CONV_A // LANES, HALO_A + tl, LANES), F32),
         pltpu.VMEM((D_XBC // LANES, HALO_B + tl, LANES), F32),
         pltpu.VMEM((tl, D_CONV_A), F32),
         pltpu.VMEM((tl, D_CONV_A), BF16)])

    (x1,) = _seq_call(
        _ssd_kernel, "ssd", bsz, seqlen,
        [(xc, D_XBC, 0), (dt, DT_PAD, 0), (zs, D_INNER, 0), (gb, D_MODEL, 0), (oa, D_MODEL, 0),
         (x2d, D_MODEL, 0), (mod3, None, None)],
        [a_log_pad, d_skip_x, row(ssm_norm_g[0]), expand,
         w_b_out[0].astype(BF16), w_o[0].astype(BF16)],
        [(D_MODEL, F32)],
        [pltpu.VMEM((tl, D_INNER), F32), pltpu.VMEM((D_STATE, D_INNER), F32),
         pltpu.VMEM((tl, D_INNER), BF16), pltpu.VMEM((tl, D_MODEL), BF16)])

    (out,) = _seq_call(
        _ffn_kernel, "ffn", bsz, seqlen,
        [(x1, D_MODEL, 0), (mod3, None, None)],
        [row(norm2_g[0]), w_up[0].astype(BF16), conv_ffn_w[0], row(conv_ffn_b[0]),
         w_down[0].astype(BF16), row(norm_f_g)],
        [(D_MODEL, F32)],
        [pltpu.VMEM((2 * D_FF // LANES, HALO_F + tl, LANES), F32)])
    return out.reshape(bsz, seqlen, D_MODEL)
```

```python
import functools

import jax
import jax.numpy as jnp
from jax import lax
from jax.experimental import pallas as pl
from jax.experimental.pallas import tpu as pltpu

F32 = jnp.float32
BF16 = jnp.bfloat16

D_MODEL = 1024
D_CONV_A = D_MODEL
KERNEL_A = 31
D_INNER = 2 * D_MODEL
HEAD_DIM = 64
N_HEADS = D_INNER // HEAD_DIM
N_GROUPS = 4
HEADS_PER_GROUP = N_HEADS // N_GROUPS
D_STATE = 128
KERNEL_SSM = 4
D_BC = N_GROUPS * D_STATE
D_XBC = D_INNER + 2 * D_BC
D_FF = 2816
KERNEL_FFN = 3
N_MOD = 6
EPS = 1e-6
EPS_SSM_NORM = 1e-5
LOG2_E = 1.4426950408889634

LANES = 128
SUBLANES = 8
DT_PAD = LANES
SSD_CHUNK = 128
GROUP_W = HEADS_PER_GROUP * HEAD_DIM

C_AVAL = 0
C_AGATE = C_AVAL + D_CONV_A
C_Z = C_AGATE + D_CONV_A
C_XBC = C_Z + D_INNER
C_DT = C_XBC + D_XBC

VMEM_LIMIT = 56 * 1024 * 1024
SEQ_TILE = 256
SSD_TILE = 512


def _dot(a, b):
    return jnp.dot(a, b, preferred_element_type=F32)


def _sigmoid(x):
    return jax.nn.sigmoid(x)


def _silu(x):
    return x * _sigmoid(x)


def _resident(shape):
    nd = len(shape)
    return pl.BlockSpec(shape, lambda *_: (0,) * nd, pipeline_mode=pl.Buffered(1))


def _seq_call(body, name, bsz, seqlen, tiled_in, resident_in, outs, scratch, tl=SEQ_TILE):
    per_seq = seqlen // tl
    in_specs, args = [], []
    for arr, cols, cidx in tiled_in:
        if cols is None:
            in_specs.append(pl.BlockSpec((1,) + arr.shape[1:], lambda b, j: (b, 0, 0)))
        else:
            in_specs.append(pl.BlockSpec((tl, cols), lambda b, j, c=cidx: (b * per_seq + j, c)))
        args.append(arr)
    for arr in resident_in:
        in_specs.append(_resident(arr.shape))
        args.append(arr)
    row = lambda b, j: (b * per_seq + j, 0)
    return pl.pallas_call(
        functools.partial(body, tl=tl),
        grid=(bsz, per_seq),
        in_specs=in_specs,
        out_specs=[pl.BlockSpec((tl, cols), row) for cols, _ in outs],
        out_shape=[jax.ShapeDtypeStruct((bsz * seqlen, cols), dt) for cols, dt in outs],
        scratch_shapes=scratch,
        compiler_params=pltpu.CompilerParams(
            dimension_semantics=("arbitrary", "arbitrary"), vmem_limit_bytes=VMEM_LIMIT),
        name=name,
    )(*args)


def _ada_kernel(c_ref, w_ref, b_ref, o_ref):
    s = _silu(c_ref[...]).astype(BF16)
    o_ref[...] = _dot(s, w_ref[...].astype(BF16)) + b_ref[...]


def _ada(c, w_ada, b_ada):
    bsz = c.shape[0]
    n = w_ada.shape[1]
    tn = D_MODEL
    return pl.pallas_call(
        _ada_kernel,
        grid=(n // tn,),
        in_specs=[
            pl.BlockSpec((bsz, D_MODEL), lambda j: (0, 0)),
            pl.BlockSpec((D_MODEL, tn), lambda j: (0, j)),
            pl.BlockSpec((1, tn), lambda j: (0, j)),
        ],
        out_specs=pl.BlockSpec((bsz, tn), lambda j: (0, j)),
        out_shape=jax.ShapeDtypeStruct((bsz, n), F32),
        compiler_params=pltpu.CompilerParams(dimension_semantics=("arbitrary",)),
        name="ada",
    )(c, w_ada, b_ada.reshape(1, n))


def _carry_history(ext_ref, halo, tl, first_tile):
    @pl.when(first_tile)
    def _():
        ext_ref[:, 0:halo, :] = jnp.zeros((ext_ref.shape[0], halo, LANES), ext_ref.dtype)

    @pl.when(jnp.logical_not(first_tile))
    def _():
        ext_ref[:, 0:halo, :] = ext_ref[:, tl:tl + halo, :]


def _fill_slabs(ext_ref, halo, tl, value, first_slab=0, r0=0):
    rows = value.shape[0]
    for i in range(value.shape[1] // LANES):
        ext_ref[first_slab + i, halo + r0:halo + r0 + rows, :] = value[:, i * LANES:(i + 1) * LANES]


def _causal_conv_slab(ext_ref, slab, w_ref, b_ref, width, halo, r0, rows):
    c0 = slab * LANES
    base = r0 + halo - (width - 1)
    acc = jnp.broadcast_to(b_ref[:, c0:c0 + LANES], (rows, LANES))
    for k in range(width):
        acc = acc + w_ref[k:k + 1, c0:c0 + LANES] * ext_ref[slab, base + k:base + k + rows, :]
    return acc


HALO_A = 32
HALO_B = SUBLANES
CONV_ROWS = 64
MXU_TILE = 256
MXU_TILE_COST = 256


def _emit_interleaved(mxu_tasks, valu_tasks):
    t_m = t_v = 0
    vi = 0
    for mi, (fn, cost, epi, need_v) in enumerate(mxu_tasks):
        while vi < need_v:
            valu_tasks[vi][0]()
            t_v += valu_tasks[vi][1]
            vi += 1
        t_v = max(t_v, t_m)
        fn()
        t_m += cost
        t_v += epi
        while (vi < len(valu_tasks) and valu_tasks[vi][2] <= mi + 1
               and t_v + valu_tasks[vi][1] // 2 <= t_m):
            valu_tasks[vi][0]()
            t_v += valu_tasks[vi][1]
            vi += 1
    for fn, _, _ in valu_tasks[vi:]:
        fn()


def _front_kernel(x_ref, mod_ref, n1g_ref, w_ref, wg_ref, bg_ref, dtb_ref,
                  cwa_ref, cba_ref, lg_ref, lb_ref, wa_ref, ba_ref, cws_ref, cbs_ref,
                  oa_ref, zs_ref, xc_ref, dt_ref, gb_ref,
                  exta_ref, extb_ref, conv_ref, s_ref, *, tl):
    first = pl.program_id(1) == 0
    _carry_history(exta_ref, HALO_A, tl, first)
    _carry_history(extb_ref, HALO_B, tl, first)

    x = x_ref[...]
    ms = jnp.mean(x * x, axis=-1, keepdims=True)
    y = x * lax.rsqrt(ms + EPS) * n1g_ref[...]
    h = (y * (1.0 + mod_ref[0, 1:2, :]) + mod_ref[0, 0:1, :]).astype(BF16)

    tn = MXU_TILE
    mxu, valu = [], []

    def proj(c0, n=tn):
        return _dot(h, w_ref[:, c0:c0 + n])

    def glu_task(p):
        def fn():
            a_val = proj(C_AVAL + p * tn)
            a_gate = proj(C_AGATE + p * tn)
            _fill_slabs(exta_ref, HALO_A, tl, a_val * _sigmoid(a_gate), p * tn // LANES)
        return fn

    def xbc_task(t):
        def fn():
            _fill_slabs(extb_ref, HALO_B, tl, proj(C_XBC + t * tn), t * tn // LANES)
        return fn

    def outa_task(t):
        def fn():
            out_a = _dot(s_ref[...], wa_ref[:, t * tn:(t + 1) * tn]) + ba_ref[:, t * tn:(t + 1) * tn]
            oa_ref[:, t * tn:(t + 1) * tn] = oa_ref[:, t * tn:(t + 1) * tn] * out_a
        return fn

    def z_task(t):
        def fn():
            zs_ref[:, t * tn:(t + 1) * tn] = _silu(proj(C_Z + t * tn)).astype(BF16)
        return fn

    def dt_task():
        dt_raw = proj(C_DT, DT_PAD) + dtb_ref[...]
        dt_ref[...] = jnp.maximum(dt_raw, 0.0) + jnp.log1p(jnp.exp(-jnp.abs(dt_raw)))

    def gate_task(t):
        def fn():
            g = _sigmoid(_dot(h, wg_ref[:, t * tn:(t + 1) * tn]) + bg_ref[:, t * tn:(t + 1) * tn])
            if t * tn < D_MODEL:
                oa_ref[:, t * tn:(t + 1) * tn] = g
            else:
                gb_ref[:, t * tn - D_MODEL:(t + 1) * tn - D_MODEL] = g.astype(BF16)
        return fn

    def conv_a_task(sl):
        def fn():
            for r0 in range(0, tl, CONV_ROWS):
                conv_ref[r0:r0 + CONV_ROWS, sl * LANES:(sl + 1) * LANES] = _causal_conv_slab(
                    exta_ref, sl, cwa_ref, cba_ref, KERNEL_A, HALO_A, r0, CONV_ROWS)
        return fn

    def ln_task(r0):
        def fn():
            v = conv_ref[r0:r0 + CONV_ROWS, :]
            mu = jnp.mean(v, axis=-1, keepdims=True)
            vc = v - mu
            var = jnp.mean(vc * vc, axis=-1, keepdims=True)
            s_ref[r0:r0 + CONV_ROWS, :] = _silu(
                vc * lax.rsqrt(var + EPS) * lg_ref[...] + lb_ref[...]).astype(BF16)
        return fn

    def conv_b_task(sl):
        def fn():
            for r0 in range(0, tl, CONV_ROWS):
                xc_ref[r0:r0 + CONV_ROWS, sl * LANES:(sl + 1) * LANES] = _silu(_causal_conv_slab(
                    extb_ref, sl, cws_ref, cbs_ref, KERNEL_SSM, HALO_B, r0, CONV_ROWS)).astype(BF16)
        return fn

    n_glu = D_CONV_A // tn
    n_xbc = D_XBC // tn
    for p in range(n_glu):
        mxu.append((glu_task(p), 2 * MXU_TILE_COST, 80, 0))
        for sl in range(p * tn // LANES, (p + 1) * tn // LANES):
            valu.append((conv_a_task(sl), 500, len(mxu)))
    for r0 in range(0, tl, CONV_ROWS):
        valu.append((ln_task(r0), 200, len(mxu)))
    n_ln_done = len(valu)
    for t in range(n_xbc):
        mxu.append((xbc_task(t), MXU_TILE_COST, 0, 0))
        for sl in range(t * tn // LANES, (t + 1) * tn // LANES):
            valu.append((conv_b_task(sl), 115, len(mxu)))
    for t in range(D_INNER // tn):
        mxu.append((z_task(t), MXU_TILE_COST, 80, 0))
    mxu.append((dt_task, MXU_TILE_COST // 2, 40, 0))
    for t in range(2 * D_MODEL // tn):
        mxu.append((gate_task(t), MXU_TILE_COST, 80, 0))
    for t in range(D_MODEL // tn):
        mxu.append((outa_task(t), MXU_TILE_COST, 40, n_ln_done))
    _emit_interleaved(mxu, valu)


def _ssd_kernel(xc_ref, dt_ref, zs_ref, gb_ref, oa_ref, x_ref, mod_ref,
                alog_ref, dsk_ref, ng_ref, e_ref, wb_ref, wo_ref,
                o_ref, y_ref, s_ref, yn_ref, mg_ref, *, tl):
    @pl.when(pl.program_id(1) == 0)
    def _():
        s_ref[...] = jnp.zeros(s_ref.shape, s_ref.dtype)

    q = SSD_CHUNK
    a_neg = -jnp.exp(alog_ref[...])
    rows = lax.broadcasted_iota(jnp.int32, (q, q), 0)
    cols = lax.broadcasted_iota(jnp.int32, (q, q), 1)
    causal = rows >= cols
    tril_b = causal.astype(BF16)
    lane = lax.broadcasted_iota(jnp.int32, (q, LANES), 1)
    left = lane < HEAD_DIM
    expand = e_ref[...]

    def chunk_prep(ci):
        r0 = ci * q
        dtq = dt_ref[r0:r0 + q, :]
        da = dtq * a_neg
        da_hi = da.astype(BF16)
        rem = da - da_hi.astype(F32)
        da_mid = rem.astype(BF16)
        da_lo = (rem - da_mid.astype(F32)).astype(BF16)
        csum = _dot(tril_b, jnp.concatenate([da_hi, da_mid, da_lo], axis=1))
        acum = (csum[:, 0:DT_PAD] + csum[:, DT_PAD:2 * DT_PAD]) + csum[:, 2 * DT_PAD:3 * DT_PAD]
        total = acum[q - 1:q, :]
        w_state = jnp.exp(total - acum) * dtq
        col2 = acum * LOG2_E
        row2_t = (col2 - jnp.log2(dtq)).T
        return dict(
            col2=col2, row2_t=row2_t,
            w_state_x=_dot(w_state.astype(BF16), expand),
            decay_in_x=_dot(jnp.exp(acum).astype(BF16), expand))

    def group_head(ci, g, ck):
        r0 = ci * q
        c0 = g * GROUP_W
        bg = xc_ref[r0:r0 + q, D_INNER + g * D_STATE:D_INNER + (g + 1) * D_STATE]
        cg = xc_ref[r0:r0 + q, D_INNER + D_BC + g * D_STATE:D_INNER + D_BC + (g + 1) * D_STATE]
        cb_mat = lax.dot_general(cg, bg, (((1,), (1,)), ((), ())),
                                 preferred_element_type=F32)
        y_off = _dot(cg, s_ref[:, c0:c0 + GROUP_W].astype(BF16))
        return cb_mat, y_off

    def group_tail(ci, g, ck, cb_mat, y_off):
        r0 = ci * q
        c0 = g * GROUP_W
        col2, row2_t = ck["col2"], ck["row2_t"]
        decay_in_x = ck["decay_in_x"][:, c0:c0 + GROUP_W]
        y_off = y_off * decay_in_x
        for pr in range(HEADS_PER_GROUP // 2):
            h1 = g * HEADS_PER_GROUP + 2 * pr

            def scores(h):
                seg2 = col2[:, h:h + 1] - row2_t[h:h + 1, :]
                return (jnp.where(causal, jnp.exp2(seg2), 0.0) * cb_mat).astype(BF16)

            lhs = jnp.concatenate([scores(h1), scores(h1 + 1)], axis=1)
            xp = xc_ref[r0:r0 + q, c0 + pr * LANES:c0 + (pr + 1) * LANES]
            zero = jnp.zeros_like(xp)
            rhs = jnp.concatenate([jnp.where(left, xp, zero),
                                   jnp.where(left, zero, xp)], axis=0)
            y_ref[r0:r0 + q, c0 + pr * LANES:c0 + (pr + 1) * LANES] = (
                _dot(lhs, rhs) + y_off[:, pr * LANES:(pr + 1) * LANES])

        bg = xc_ref[r0:r0 + q, D_INNER + g * D_STATE:D_INNER + (g + 1) * D_STATE]
        xs_b = xc_ref[r0:r0 + q, c0:c0 + GROUP_W]
        xw = (xs_b.astype(F32) * ck["w_state_x"][:, c0:c0 + GROUP_W]).astype(BF16)
        new_state = lax.dot_general(bg, xw, (((0,), (0,)), ((), ())),
                                    preferred_element_type=F32)
        s_ref[:, c0:c0 + GROUP_W] = s_ref[:, c0:c0 + GROUP_W] * decay_in_x[q - 1:q, :] + new_state

    n_chunks = tl // q
    ck = chunk_prep(0)
    head = group_head(0, 0, ck)
    for ci in range(n_chunks):
        ck_next = None
        for g in range(N_GROUPS):
            cur = head
            if g + 1 < N_GROUPS:
                head = group_head(ci, g + 1, ck)
            elif ci + 1 < n_chunks:
                ck_next = chunk_prep(ci + 1)
            group_tail(ci, g, ck, *cur)
        if ci + 1 < n_chunks:
            ck = ck_next
            head = group_head(ci + 1, 0, ck)

    y = y_ref[...] + xc_ref[:, 0:D_INNER].astype(F32) * dsk_ref[...]
    y = y * zs_ref[...].astype(F32)
    for g in range(N_GROUPS):
        yg = y[:, g * GROUP_W:(g + 1) * GROUP_W]
        ms = jnp.mean(yg * yg, axis=-1, keepdims=True)
        yn_ref[:, g * GROUP_W:(g + 1) * GROUP_W] = (
            yg * lax.rsqrt(ms + EPS_SSM_NORM) * ng_ref[:, g * GROUP_W:(g + 1) * GROUP_W]).astype(BF16)
    out_b = _dot(yn_ref[...], wb_ref[...])
    mg_ref[...] = (oa_ref[...] + gb_ref[...].astype(F32) * out_b).astype(BF16)
    mix = _dot(mg_ref[...], wo_ref[...])
    o_ref[...] = x_ref[...] + mod_ref[0, 2:3, :] * mix


HALO_F = SUBLANES
COLS_F = 256
FFN_TILE = 512
FFN_HALVES = 2


def _ffn_kernel(x_ref, mod_ref, n2g_ref, wup_ref, cw_ref, cb_ref, wdn_ref, nfg_ref,
                o_ref, ext_ref, h2_ref, *, tl):
    _carry_history(ext_ref, HALO_F, tl, pl.program_id(1) == 0)
    hr = tl // FFN_HALVES

    def rows(hf):
        return slice(hf * hr, (hf + 1) * hr)

    def prologue(hf):
        x = x_ref[rows(hf), :]
        ms = jnp.mean(x * x, axis=-1, keepdims=True)
        y = x * lax.rsqrt(ms + EPS) * n2g_ref[...]
        h2_ref[rows(hf), :] = (y * (1.0 + mod_ref[0, 4:5, :]) + mod_ref[0, 3:4, :]).astype(BF16)

    def up_cols(hf, blk):
        for c0 in (blk * COLS_F, D_FF + blk * COLS_F):
            _fill_slabs(ext_ref, HALO_F, tl, _dot(h2_ref[rows(hf), :], wup_ref[:, c0:c0 + COLS_F]),
                        c0 // LANES, hf * hr)

    def act_cols(hf, blk):
        def conv(c0):
            return jnp.concatenate(
                [_causal_conv_slab(ext_ref, c0 // LANES + i, cw_ref, cb_ref, KERNEL_FFN, HALO_F, hf * hr, hr)
                 for i in range(COLS_F // LANES)], axis=1)
        return (_silu(conv(blk * COLS_F)) * conv(D_FF + blk * COLS_F)).astype(BF16)

    def epilogue(hf, down):
        xo = x_ref[rows(hf), :] + mod_ref[0, 5:6, :] * down
        ms2 = jnp.mean(xo * xo, axis=-1, keepdims=True)
        o_ref[rows(hf), :] = xo * lax.rsqrt(ms2 + EPS) * nfg_ref[...]

    nblk = D_FF // COLS_F
    units = [(hf, blk) for hf in range(FFN_HALVES) for blk in range(nblk)]
    down = [jnp.zeros((hr, D_MODEL), F32) for _ in range(FFN_HALVES)]
    prologue(0)
    up_cols(*units[0])
    for i, (hf, blk) in enumerate(units):
        if hf + 1 < FFN_HALVES and blk == nblk // 2:
            prologue(hf + 1)
        if i + 1 < len(units):
            up_cols(*units[i + 1])
        down[hf] = down[hf] + _dot(act_cols(hf, blk), wdn_ref[blk * COLS_F:(blk + 1) * COLS_F, :])
        if hf > 0 and blk == 1:
            epilogue(hf - 1, down[hf - 1])
    epilogue(FFN_HALVES - 1, down[FFN_HALVES - 1])


def kernel(x, c, w_ada, b_ada, norm1_g, w_in, conv_a_w, conv_a_b, ln_a_g, ln_a_b, w_a_out, b_a_out, conv_ssm_w, conv_ssm_b, dt_bias, a_log, d_skip, ssm_norm_g, w_b_out, w_gate, b_gate, w_o, norm2_g, w_up, conv_ffn_w, conv_ffn_b, w_down, norm_f_g):
    bsz, seqlen, d = x.shape
    assert d == D_MODEL and w_ada.shape[0] == 1
    tl = SEQ_TILE
    assert seqlen % SEQ_TILE == 0 and seqlen % SSD_TILE == 0 and seqlen % FFN_TILE == 0
    assert SSD_TILE % SSD_CHUNK == 0 and FFN_TILE % FFN_HALVES == 0

    row = lambda v: v.reshape(1, -1)
    n_dt = N_HEADS
    w_in_p = jnp.pad(w_in[0], ((0, 0), (0, DT_PAD - n_dt))).astype(BF16)
    dt_bias_pad = jnp.pad(dt_bias[0], (0, DT_PAD - n_dt)).reshape(1, DT_PAD)
    a_log_pad = jnp.pad(a_log[0], (0, DT_PAD - n_dt)).reshape(1, DT_PAD)
    d_skip_x = jnp.repeat(d_skip[0], HEAD_DIM).reshape(1, D_INNER)
    expand = (lax.broadcasted_iota(jnp.int32, (DT_PAD, D_INNER), 0)
              == lax.broadcasted_iota(jnp.int32, (DT_PAD, D_INNER), 1) // HEAD_DIM).astype(BF16)

    x2d = x.reshape(bsz * seqlen, D_MODEL)
    mod3 = _ada(c, w_ada[0], b_ada[0]).reshape(bsz, N_MOD, D_MODEL)

    oa, zs, xc, dt, gb = _seq_call(
        _front_kernel, "front", bsz, seqlen,
        [(x2d, D_MODEL, 0), (mod3, None, None)],
        [row(norm1_g[0]), w_in_p, w_gate[0].astype(BF16), row(b_gate[0]), dt_bias_pad,
         conv_a_w[0], row(conv_a_b[0]), row(ln_a_g[0]), row(ln_a_b[0]),
         w_a_out[0].astype(BF16), row(b_a_out[0]), conv_ssm_w[0], row(conv_ssm_b[0])],
        [(D_MODEL, F32), (D_INNER, BF16), (D_XBC, BF16), (DT_PAD, F32), (D_MODEL, BF16)],
        [pltpu.VMEM((D_CONV_A // LANES, HALO_A + tl, LANES), F32),
         pltpu.VMEM((D_XBC // LANES, HALO_B + tl, LANES), F32),
         pltpu.VMEM((tl, D_CONV_A), F32),
         pltpu.VMEM((tl, D_CONV_A), BF16)])

    (x1,) = _seq_call(
        _ssd_kernel, "ssd", bsz, seqlen,
        [(xc, D_XBC, 0), (dt, DT_PAD, 0), (zs, D_INNER, 0), (gb, D_MODEL, 0), (oa, D_MODEL, 0),
         (x2d, D_MODEL, 0), (mod3, None, None)],
        [a_log_pad, d_skip_x, row(ssm_norm_g[0]), expand,
         w_b_out[0].astype(BF16), w_o[0].astype(BF16)],
        [(D_MODEL, F32)],
        [pltpu.VMEM((SSD_TILE, D_INNER), F32), pltpu.VMEM((D_STATE, D_INNER), F32),
         pltpu.VMEM((SSD_TILE, D_INNER), BF16), pltpu.VMEM((SSD_TILE, D_MODEL), BF16)], tl=SSD_TILE)

    (out,) = _seq_call(
        _ffn_kernel, "ffn", bsz, seqlen,
        [(x1, D_MODEL, 0), (mod3, None, None)],
        [row(norm2_g[0]), w_up[0].astype(BF16), conv_ffn_w[0], row(conv_ffn_b[0]),
         w_down[0].astype(BF16), row(norm_f_g)],
        [(D_MODEL, F32)],
        [pltpu.VMEM((2 * D_FF // LANES, HALO_F + FFN_TILE, LANES), F32),
         pltpu.VMEM((FFN_TILE, D_MODEL), BF16)], tl=FFN_TILE)
    return out.reshape(bsz, seqlen, D_MODEL)
```

```python
import functools

import jax
import jax.numpy as jnp
from jax import lax
from jax.experimental import pallas as pl
from jax.experimental.pallas import tpu as pltpu

F32 = jnp.float32
BF16 = jnp.bfloat16

D_MODEL = 1024
D_CONV_A = D_MODEL
KERNEL_A = 31
D_INNER = 2 * D_MODEL
HEAD_DIM = 64
N_HEADS = D_INNER // HEAD_DIM
N_GROUPS = 4
HEADS_PER_GROUP = N_HEADS // N_GROUPS
D_STATE = 128
KERNEL_SSM = 4
D_BC = N_GROUPS * D_STATE
D_XBC = D_INNER + 2 * D_BC
D_FF = 2816
KERNEL_FFN = 3
N_MOD = 6
EPS = 1e-6
EPS_SSM_NORM = 1e-5
LOG2_E = 1.4426950408889634

LANES = 128
SUBLANES = 8
DT_PAD = LANES
SSD_CHUNK = 128
GROUP_W = HEADS_PER_GROUP * HEAD_DIM

C_AVAL = 0
C_AGATE = C_AVAL + D_CONV_A
C_Z = C_AGATE + D_CONV_A
C_XBC = C_Z + D_INNER
C_DT = C_XBC + D_XBC

VMEM_LIMIT = 56 * 1024 * 1024
SEQ_TILE = 256
SSD_TILE = 512


def _dot(a, b):
    return jnp.dot(a, b, preferred_element_type=F32)


def _sigmoid(x):
    return jax.nn.sigmoid(x)


def _silu(x):
    return x * _sigmoid(x)


def _resident(shape):
    nd = len(shape)
    return pl.BlockSpec(shape, lambda *_: (0,) * nd, pipeline_mode=pl.Buffered(1))


def _seq_call(body, name, bsz, seqlen, tiled_in, resident_in, outs, scratch, tl=SEQ_TILE):
    per_seq = seqlen // tl
    in_specs, args = [], []
    for arr, cols, cidx in tiled_in:
        if cols is None:
            in_specs.append(pl.BlockSpec((1,) + arr.shape[1:], lambda b, j: (b, 0, 0)))
        else:
            in_specs.append(pl.BlockSpec((tl, cols), lambda b, j, c=cidx: (b * per_seq + j, c)))
        args.append(arr)
    for arr in resident_in:
        in_specs.append(_resident(arr.shape))
        args.append(arr)
    row = lambda b, j: (b * per_seq + j, 0)
    return pl.pallas_call(
        functools.partial(body, tl=tl),
        grid=(bsz, per_seq),
        in_specs=in_specs,
        out_specs=[pl.BlockSpec((tl, cols), row) for cols, _ in outs],
        out_shape=[jax.ShapeDtypeStruct((bsz * seqlen, cols), dt) for cols, dt in outs],
        scratch_shapes=scratch,
        compiler_params=pltpu.CompilerParams(
            dimension_semantics=("arbitrary", "arbitrary"), vmem_limit_bytes=VMEM_LIMIT),
        name=name,
    )(*args)


def _ada_kernel(c_ref, w_ref, b_ref, o_ref):
    s = _silu(c_ref[...]).astype(BF16)
    o_ref[...] = _dot(s, w_ref[...].astype(BF16)) + b_ref[...]


def _ada(c, w_ada, b_ada):
    bsz = c.shape[0]
    n = w_ada.shape[1]
    tn = D_MODEL
    return pl.pallas_call(
        _ada_kernel,
        grid=(n // tn,),
        in_specs=[
            pl.BlockSpec((bsz, D_MODEL), lambda j: (0, 0)),
            pl.BlockSpec((D_MODEL, tn), lambda j: (0, j)),
            pl.BlockSpec((1, tn), lambda j: (0, j)),
        ],
        out_specs=pl.BlockSpec((bsz, tn), lambda j: (0, j)),
        out_shape=jax.ShapeDtypeStruct((bsz, n), F32),
        compiler_params=pltpu.CompilerParams(dimension_semantics=("arbitrary",)),
        name="ada",
    )(c, w_ada, b_ada.reshape(1, n))


def _carry_history(ext_ref, halo, tl, first_tile):
    @pl.when(first_tile)
    def _():
        ext_ref[:, 0:halo, :] = jnp.zeros((ext_ref.shape[0], halo, LANES), ext_ref.dtype)

    @pl.when(jnp.logical_not(first_tile))
    def _():
        ext_ref[:, 0:halo, :] = ext_ref[:, tl:tl + halo, :]


def _fill_slabs(ext_ref, halo, tl, value, first_slab=0, r0=0):
    rows = value.shape[0]
    for i in range(value.shape[1] // LANES):
        ext_ref[first_slab + i, halo + r0:halo + r0 + rows, :] = value[:, i * LANES:(i + 1) * LANES]


def _causal_conv_slab(ext_ref, slab, w_ref, b_ref, width, halo, r0, rows):
    c0 = slab * LANES
    base = r0 + halo - (width - 1)
    acc = jnp.broadcast_to(b_ref[:, c0:c0 + LANES], (rows, LANES))
    for k in range(width):
        acc = acc + w_ref[k:k + 1, c0:c0 + LANES] * ext_ref[slab, base + k:base + k + rows, :]
    return acc


HALO_A = 32
HALO_B = SUBLANES
CONV_ROWS = 64
MXU_TILE = 256
MXU_TILE_COST = 256


def _emit_interleaved(mxu_tasks, valu_tasks):
    t_m = t_v = 0
    vi = 0
    for mi, (fn, cost, epi, need_v) in enumerate(mxu_tasks):
        while vi < need_v:
            valu_tasks[vi][0]()
            t_v += valu_tasks[vi][1]
            vi += 1
        t_v = max(t_v, t_m)
        fn()
        t_m += cost
        t_v += epi
        while (vi < len(valu_tasks) and valu_tasks[vi][2] <= mi + 1
               and t_v + valu_tasks[vi][1] // 2 <= t_m):
            valu_tasks[vi][0]()
            t_v += valu_tasks[vi][1]
            vi += 1
    for fn, _, _ in valu_tasks[vi:]:
        fn()


def _front_kernel(x_ref, mod_ref, n1g_ref, w_ref, wg_ref, bg_ref, dtb_ref,
                  cwa_ref, cba_ref, lg_ref, lb_ref, wa_ref, ba_ref, cws_ref, cbs_ref,
                  oa_ref, h_ref, xc_ref, dt_ref,
                  exta_ref, extb_ref, conv_ref, s_ref, *, tl):
    first = pl.program_id(1) == 0
    _carry_history(exta_ref, HALO_A, tl, first)
    _carry_history(extb_ref, HALO_B, tl, first)

    x = x_ref[...]
    ms = jnp.mean(x * x, axis=-1, keepdims=True)
    y = x * lax.rsqrt(ms + EPS) * n1g_ref[...]
    h = (y * (1.0 + mod_ref[0, 1:2, :]) + mod_ref[0, 0:1, :]).astype(BF16)
    h_ref[...] = h

    tn = MXU_TILE
    mxu, valu = [], []

    def proj(c0, n=tn):
        return _dot(h, w_ref[:, c0:c0 + n])

    def glu_task(p):
        def fn():
            a_val = proj(C_AVAL + p * tn)
            a_gate = proj(C_AGATE + p * tn)
            _fill_slabs(exta_ref, HALO_A, tl, a_val * _sigmoid(a_gate), p * tn // LANES)
        return fn

    def xbc_task(t):
        def fn():
            _fill_slabs(extb_ref, HALO_B, tl, proj(C_XBC + t * tn), t * tn // LANES)
        return fn

    def outa_task(t):
        def fn():
            out_a = _dot(s_ref[...], wa_ref[:, t * tn:(t + 1) * tn]) + ba_ref[:, t * tn:(t + 1) * tn]
            oa_ref[:, t * tn:(t + 1) * tn] = oa_ref[:, t * tn:(t + 1) * tn] * out_a
        return fn

    def dt_task():
        dt_raw = proj(C_DT, DT_PAD) + dtb_ref[...]
        dt_ref[...] = jnp.maximum(dt_raw, 0.0) + jnp.log1p(jnp.exp(-jnp.abs(dt_raw)))

    def gate_task(t):
        def fn():
            oa_ref[:, t * tn:(t + 1) * tn] = _sigmoid(
                _dot(h, wg_ref[:, t * tn:(t + 1) * tn]) + bg_ref[:, t * tn:(t + 1) * tn])
        return fn

    def conv_a_task(sl):
        def fn():
            for r0 in range(0, tl, CONV_ROWS):
                conv_ref[r0:r0 + CONV_ROWS, sl * LANES:(sl + 1) * LANES] = _causal_conv_slab(
                    exta_ref, sl, cwa_ref, cba_ref, KERNEL_A, HALO_A, r0, CONV_ROWS)
        return fn

    def ln_task(r0):
        def fn():
            v = conv_ref[r0:r0 + CONV_ROWS, :]
            mu = jnp.mean(v, axis=-1, keepdims=True)
            vc = v - mu
            var = jnp.mean(vc * vc, axis=-1, keepdims=True)
            s_ref[r0:r0 + CONV_ROWS, :] = _silu(
                vc * lax.rsqrt(var + EPS) * lg_ref[...] + lb_ref[...]).astype(BF16)
        return fn

    def conv_b_task(sl):
        def fn():
            for r0 in range(0, tl, CONV_ROWS):
                xc_ref[r0:r0 + CONV_ROWS, sl * LANES:(sl + 1) * LANES] = _silu(_causal_conv_slab(
                    extb_ref, sl, cws_ref, cbs_ref, KERNEL_SSM, HALO_B, r0, CONV_ROWS)).astype(BF16)
        return fn

    n_glu = D_CONV_A // tn
    n_xbc = D_XBC // tn
    for p in range(n_glu):
        mxu.append((glu_task(p), 2 * MXU_TILE_COST, 80, 0))
        for sl in range(p * tn // LANES, (p + 1) * tn // LANES):
            valu.append((conv_a_task(sl), 500, len(mxu)))
    for r0 in range(0, tl, CONV_ROWS):
        valu.append((ln_task(r0), 200, len(mxu)))
    n_ln_done = len(valu)
    for t in range(n_xbc):
        mxu.append((xbc_task(t), MXU_TILE_COST, 0, 0))
        for sl in range(t * tn // LANES, (t + 1) * tn // LANES):
            valu.append((conv_b_task(sl), 115, len(mxu)))
    mxu.append((dt_task, MXU_TILE_COST // 2, 40, 0))
    for t in range(D_MODEL // tn):
        mxu.append((gate_task(t), MXU_TILE_COST, 80, 0))
    for t in range(D_MODEL // tn):
        mxu.append((outa_task(t), MXU_TILE_COST, 40, n_ln_done))
    _emit_interleaved(mxu, valu)


def _ssd_kernel(xc_ref, dt_ref, h_ref, oa_ref, x_ref, mod_ref,
                alog_ref, dsk_ref, ng_ref, e_ref, wb_ref, wo_ref, wz_ref, wgb_ref, bgb_ref,
                o_ref, y_ref, s_ref, yn_ref, mg_ref, zs_ref, gb_ref, *, tl):
    @pl.when(pl.program_id(1) == 0)
    def _():
        s_ref[...] = jnp.zeros(s_ref.shape, s_ref.dtype)

    q = SSD_CHUNK
    a_neg = -jnp.exp(alog_ref[...])
    rows = lax.broadcasted_iota(jnp.int32, (q, q), 0)
    cols = lax.broadcasted_iota(jnp.int32, (q, q), 1)
    causal = rows >= cols
    tril_b = causal.astype(BF16)
    lane = lax.broadcasted_iota(jnp.int32, (q, LANES), 1)
    left = lane < HEAD_DIM
    expand = e_ref[...]

    def chunk_prep(ci):
        r0 = ci * q
        dtq = dt_ref[r0:r0 + q, :]
        da = dtq * a_neg
        da_hi = da.astype(BF16)
        rem = da - da_hi.astype(F32)
        da_mid = rem.astype(BF16)
        da_lo = (rem - da_mid.astype(F32)).astype(BF16)
        csum = _dot(tril_b, jnp.concatenate([da_hi, da_mid, da_lo], axis=1))
        acum = (csum[:, 0:DT_PAD] + csum[:, DT_PAD:2 * DT_PAD]) + csum[:, 2 * DT_PAD:3 * DT_PAD]
        total = acum[q - 1:q, :]
        w_state = jnp.exp(total - acum) * dtq
        col2 = acum * LOG2_E
        row2_t = (col2 - jnp.log2(dtq)).T
        return dict(
            col2=col2, row2_t=row2_t,
            w_state_x=_dot(w_state.astype(BF16), expand),
            decay_in_x=_dot(jnp.exp(acum).astype(BF16), expand))

    def group_head(ci, g, ck):
        r0 = ci * q
        c0 = g * GROUP_W
        bg = xc_ref[r0:r0 + q, D_INNER + g * D_STATE:D_INNER + (g + 1) * D_STATE]
        cg = xc_ref[r0:r0 + q, D_INNER + D_BC + g * D_STATE:D_INNER + D_BC + (g + 1) * D_STATE]
        cb_mat = lax.dot_general(cg, bg, (((1,), (1,)), ((), ())),
                                 preferred_element_type=F32)
        y_off = _dot(cg, s_ref[:, c0:c0 + GROUP_W].astype(BF16))
        return cb_mat, y_off

    def group_tail(ci, g, ck, cb_mat, y_off):
        r0 = ci * q
        c0 = g * GROUP_W
        col2, row2_t = ck["col2"], ck["row2_t"]
        decay_in_x = ck["decay_in_x"][:, c0:c0 + GROUP_W]
        y_off = y_off * decay_in_x
        for pr in range(HEADS_PER_GROUP // 2):
            h1 = g * HEADS_PER_GROUP + 2 * pr

            def scores(h):
                seg2 = col2[:, h:h + 1] - row2_t[h:h + 1, :]
                return (jnp.where(causal, jnp.exp2(seg2), 0.0) * cb_mat).astype(BF16)

            lhs = jnp.concatenate([scores(h1), scores(h1 + 1)], axis=1)
            xp = xc_ref[r0:r0 + q, c0 + pr * LANES:c0 + (pr + 1) * LANES]
            zero = jnp.zeros_like(xp)
            rhs = jnp.concatenate([jnp.where(left, xp, zero),
                                   jnp.where(left, zero, xp)], axis=0)
            y_ref[r0:r0 + q, c0 + pr * LANES:c0 + (pr + 1) * LANES] = (
                _dot(lhs, rhs) + y_off[:, pr * LANES:(pr + 1) * LANES])

        bg = xc_ref[r0:r0 + q, D_INNER + g * D_STATE:D_INNER + (g + 1) * D_STATE]
        xs_b = xc_ref[r0:r0 + q, c0:c0 + GROUP_W]
        xw = (xs_b.astype(F32) * ck["w_state_x"][:, c0:c0 + GROUP_W]).astype(BF16)
        new_state = lax.dot_general(bg, xw, (((0,), (0,)), ((), ())),
                                    preferred_element_type=F32)
        s_ref[:, c0:c0 + GROUP_W] = s_ref[:, c0:c0 + GROUP_W] * decay_in_x[q - 1:q, :] + new_state

    tn = MXU_TILE
    side = []
    for t in range(D_INNER // tn):
        def z_tile(t=t):
            zs_ref[:, t * tn:(t + 1) * tn] = _silu(_dot(h_ref[...], wz_ref[:, t * tn:(t + 1) * tn])).astype(BF16)
        side.append(z_tile)
    for t in range(D_MODEL // tn):
        def gb_tile(t=t):
            gb_ref[:, t * tn:(t + 1) * tn] = _sigmoid(
                _dot(h_ref[...], wgb_ref[:, t * tn:(t + 1) * tn]) + bgb_ref[:, t * tn:(t + 1) * tn]).astype(BF16)
        side.append(gb_tile)

    n_chunks = tl // q
    ck = chunk_prep(0)
    head = group_head(0, 0, ck)
    for ci in range(n_chunks):
        ck_next = None
        for g in range(N_GROUPS):
            cur = head
            if g + 1 < N_GROUPS:
                head = group_head(ci, g + 1, ck)
            elif ci + 1 < n_chunks:
                ck_next = chunk_prep(ci + 1)
            if side:
                side.pop(0)()
            group_tail(ci, g, ck, *cur)
        if ci + 1 < n_chunks:
            ck = ck_next
            head = group_head(ci + 1, 0, ck)

    while side:
        side.pop(0)()
    y = y_ref[...] + xc_ref[:, 0:D_INNER].astype(F32) * dsk_ref[...]
    y = y * zs_ref[...].astype(F32)
    for g in range(N_GROUPS):
        yg = y[:, g * GROUP_W:(g + 1) * GROUP_W]
        ms = jnp.mean(yg * yg, axis=-1, keepdims=True)
        yn_ref[:, g * GROUP_W:(g + 1) * GROUP_W] = (
            yg * lax.rsqrt(ms + EPS_SSM_NORM) * ng_ref[:, g * GROUP_W:(g + 1) * GROUP_W]).astype(BF16)
    out_b = _dot(yn_ref[...], wb_ref[...])
    mg_ref[...] = (oa_ref[...] + gb_ref[...].astype(F32) * out_b).astype(BF16)
    mix = _dot(mg_ref[...], wo_ref[...])
    o_ref[...] = x_ref[...] + mod_ref[0, 2:3, :] * mix


HALO_F = SUBLANES
COLS_F = 256
FFN_TILE = 512
FFN_HALVES = 2


def _ffn_kernel(x_ref, mod_ref, n2g_ref, wup_ref, cw_ref, cb_ref, wdn_ref, nfg_ref,
                o_ref, ext_ref, h2_ref, *, tl):
    _carry_history(ext_ref, HALO_F, tl, pl.program_id(1) == 0)
    hr = tl // FFN_HALVES

    def rows(hf):
        return slice(hf * hr, (hf + 1) * hr)

    def prologue(hf):
        x = x_ref[rows(hf), :]
        ms = jnp.mean(x * x, axis=-1, keepdims=True)
        y = x * lax.rsqrt(ms + EPS) * n2g_ref[...]
        h2_ref[rows(hf), :] = (y * (1.0 + mod_ref[0, 4:5, :]) + mod_ref[0, 3:4, :]).astype(BF16)

    def up_cols(hf, blk):
        for c0 in (blk * COLS_F, D_FF + blk * COLS_F):
            _fill_slabs(ext_ref, HALO_F, tl, _dot(h2_ref[rows(hf), :], wup_ref[:, c0:c0 + COLS_F]),
                        c0 // LANES, hf * hr)

    def act_cols(hf, blk):
        def conv(c0):
            return jnp.concatenate(
                [_causal_conv_slab(ext_ref, c0 // LANES + i, cw_ref, cb_ref, KERNEL_FFN, HALO_F, hf * hr, hr)
                 for i in range(COLS_F // LANES)], axis=1)
        return (_silu(conv(blk * COLS_F)) * conv(D_FF + blk * COLS_F)).astype(BF16)

    def epilogue(hf, down):
        xo = x_ref[rows(hf), :] + mod_ref[0, 5:6, :] * down
        ms2 = jnp.mean(xo * xo, axis=-1, keepdims=True)
        o_ref[rows(hf), :] = xo * lax.rsqrt(ms2 + EPS) * nfg_ref[...]

    nblk = D_FF // COLS_F
    units = [(hf, blk) for hf in range(FFN_HALVES) for blk in range(nblk)]
    down = [jnp.zeros((hr, D_MODEL), F32) for _ in range(FFN_HALVES)]
    prologue(0)
    up_cols(*units[0])
    for i, (hf, blk) in enumerate(units):
        if hf + 1 < FFN_HALVES and blk == nblk // 2:
            prologue(hf + 1)
        if i + 1 < len(units):
            up_cols(*units[i + 1])
        down[hf] = down[hf] + _dot(act_cols(hf, blk), wdn_ref[blk * COLS_F:(blk + 1) * COLS_F, :])
        if hf > 0 and blk == 1:
            epilogue(hf - 1, down[hf - 1])
    epilogue(FFN_HALVES - 1, down[FFN_HALVES - 1])


def kernel(x, c, w_ada, b_ada, norm1_g, w_in, conv_a_w, conv_a_b, ln_a_g, ln_a_b, w_a_out, b_a_out, conv_ssm_w, conv_ssm_b, dt_bias, a_log, d_skip, ssm_norm_g, w_b_out, w_gate, b_gate, w_o, norm2_g, w_up, conv_ffn_w, conv_ffn_b, w_down, norm_f_g):
    bsz, seqlen, d = x.shape
    assert d == D_MODEL and w_ada.shape[0] == 1
    tl = SEQ_TILE
    assert seqlen % SEQ_TILE == 0 and seqlen % SSD_TILE == 0 and seqlen % FFN_TILE == 0
    assert SSD_TILE % SSD_CHUNK == 0 and FFN_TILE % FFN_HALVES == 0

    row = lambda v: v.reshape(1, -1)
    n_dt = N_HEADS
    w_in_p = jnp.pad(w_in[0], ((0, 0), (0, DT_PAD - n_dt))).astype(BF16)
    dt_bias_pad = jnp.pad(dt_bias[0], (0, DT_PAD - n_dt)).reshape(1, DT_PAD)
    a_log_pad = jnp.pad(a_log[0], (0, DT_PAD - n_dt)).reshape(1, DT_PAD)
    d_skip_x = jnp.repeat(d_skip[0], HEAD_DIM).reshape(1, D_INNER)
    expand = (lax.broadcasted_iota(jnp.int32, (DT_PAD, D_INNER), 0)
              == lax.broadcasted_iota(jnp.int32, (DT_PAD, D_INNER), 1) // HEAD_DIM).astype(BF16)

    x2d = x.reshape(bsz * seqlen, D_MODEL)
    mod3 = _ada(c, w_ada[0], b_ada[0]).reshape(bsz, N_MOD, D_MODEL)

    oa, hb, xc, dt = _seq_call(
        _front_kernel, "front", bsz, seqlen,
        [(x2d, D_MODEL, 0), (mod3, None, None)],
        [row(norm1_g[0]), w_in_p, w_gate[0][:, :D_MODEL].astype(BF16), row(b_gate[0][:D_MODEL]), dt_bias_pad,
         conv_a_w[0], row(conv_a_b[0]), row(ln_a_g[0]), row(ln_a_b[0]),
         w_a_out[0].astype(BF16), row(b_a_out[0]), conv_ssm_w[0], row(conv_ssm_b[0])],
        [(D_MODEL, F32), (D_MODEL, BF16), (D_XBC, BF16), (DT_PAD, F32)],
        [pltpu.VMEM((D_CONV_A // LANES, HALO_A + tl, LANES), F32),
         pltpu.VMEM((D_XBC // LANES, HALO_B + tl, LANES), F32),
         pltpu.VMEM((tl, D_CONV_A), F32),
         pltpu.VMEM((tl, D_CONV_A), BF16)])

    (x1,) = _seq_call(
        _ssd_kernel, "ssd", bsz, seqlen,
        [(xc, D_XBC, 0), (dt, DT_PAD, 0), (hb, D_MODEL, 0), (oa, D_MODEL, 0),
         (x2d, D_MODEL, 0), (mod3, None, None)],
        [a_log_pad, d_skip_x, row(ssm_norm_g[0]), expand,
         w_b_out[0].astype(BF16), w_o[0].astype(BF16),
         w_in[0][:, C_Z:C_XBC].astype(BF16), w_gate[0][:, D_MODEL:].astype(BF16), row(b_gate[0][D_MODEL:])],
        [(D_MODEL, F32)],
        [pltpu.VMEM((SSD_TILE, D_INNER), F32), pltpu.VMEM((D_STATE, D_INNER), F32),
         pltpu.VMEM((SSD_TILE, D_INNER), BF16), pltpu.VMEM((SSD_TILE, D_MODEL), BF16),
         pltpu.VMEM((SSD_TILE, D_INNER), BF16), pltpu.VMEM((SSD_TILE, D_MODEL), BF16)], tl=SSD_TILE)

    (out,) = _seq_call(
        _ffn_kernel, "ffn", bsz, seqlen,
        [(x1, D_MODEL, 0), (mod3, None, None)],
        [row(norm2_g[0]), w_up[0].astype(BF16), conv_ffn_w[0], row(conv_ffn_b[0]),
         w_down[0].astype(BF16), row(norm_f_g)],
        [(D_MODEL, F32)],
        [pltpu.VMEM((2 * D_FF // LANES, HALO_F + FFN_TILE, LANES), F32),
         pltpu.VMEM((FFN_TILE, D_MODEL), BF16)], tl=FFN_TILE)
    return out.reshape(bsz, seqlen, D_MODEL)
```

```python
import functools

import jax
import jax.numpy as jnp
from jax import lax
from jax.experimental import pallas as pl
from jax.experimental.pallas import tpu as pltpu

F32 = jnp.float32
BF16 = jnp.bfloat16

D_MODEL = 1024
D_CONV_A = D_MODEL
KERNEL_A = 31
D_INNER = 2 * D_MODEL
HEAD_DIM = 64
N_HEADS = D_INNER // HEAD_DIM
N_GROUPS = 4
HEADS_PER_GROUP = N_HEADS // N_GROUPS
D_STATE = 128
KERNEL_SSM = 4
D_BC = N_GROUPS * D_STATE
D_XBC = D_INNER + 2 * D_BC
D_FF = 2816
KERNEL_FFN = 3
N_MOD = 6
EPS = 1e-6
EPS_SSM_NORM = 1e-5
LOG2_E = 1.4426950408889634

LANES = 128
SUBLANES = 8
DT_PAD = LANES
SSD_CHUNK = 128
GROUP_W = HEADS_PER_GROUP * HEAD_DIM

C_AVAL = 0
C_AGATE = C_AVAL + D_CONV_A
C_Z = C_AGATE + D_CONV_A
C_XBC = C_Z + D_INNER
C_DT = C_XBC + D_XBC

VMEM_LIMIT = 56 * 1024 * 1024
SEQ_TILE = 256
SSD_TILE = 512


def _dot(a, b):
    return jnp.dot(a, b, preferred_element_type=F32)


def _sigmoid(x):
    return jax.nn.sigmoid(x)


def _silu(x):
    return x * _sigmoid(x)


def _resident(shape):
    nd = len(shape)
    return pl.BlockSpec(shape, lambda *_: (0,) * nd, pipeline_mode=pl.Buffered(1))


def _seq_call(body, name, bsz, seqlen, tiled_in, resident_in, outs, scratch, tl=SEQ_TILE):
    per_seq = seqlen // tl
    in_specs, args = [], []
    for arr, cols, cidx in tiled_in:
        if cols is None:
            in_specs.append(pl.BlockSpec((1,) + arr.shape[1:], lambda b, j: (b, 0, 0)))
        else:
            in_specs.append(pl.BlockSpec((tl, cols), lambda b, j, c=cidx: (b * per_seq + j, c)))
        args.append(arr)
    for arr in resident_in:
        in_specs.append(_resident(arr.shape))
        args.append(arr)
    row = lambda b, j: (b * per_seq + j, 0)
    return pl.pallas_call(
        functools.partial(body, tl=tl),
        grid=(bsz, per_seq),
        in_specs=in_specs,
        out_specs=[pl.BlockSpec((tl, cols), row) for cols, _ in outs],
        out_shape=[jax.ShapeDtypeStruct((bsz * seqlen, cols), dt) for cols, dt in outs],
        scratch_shapes=scratch,
        compiler_params=pltpu.CompilerParams(
            dimension_semantics=("arbitrary", "arbitrary"), vmem_limit_bytes=VMEM_LIMIT),
        name=name,
    )(*args)


def _ada_kernel(c_ref, w_ref, b_ref, o_ref):
    s = _silu(c_ref[...]).astype(BF16)
    o_ref[...] = _dot(s, w_ref[...].astype(BF16)) + b_ref[...]


def _ada(c, w_ada, b_ada):
    bsz = c.shape[0]
    n = w_ada.shape[1]
    tn = D_MODEL
    return pl.pallas_call(
        _ada_kernel,
        grid=(n // tn,),
        in_specs=[
            pl.BlockSpec((bsz, D_MODEL), lambda j: (0, 0)),
            pl.BlockSpec((D_MODEL, tn), lambda j: (0, j)),
            pl.BlockSpec((1, tn), lambda j: (0, j)),
        ],
        out_specs=pl.BlockSpec((bsz, tn), lambda j: (0, j)),
        out_shape=jax.ShapeDtypeStruct((bsz, n), F32),
        compiler_params=pltpu.CompilerParams(dimension_semantics=("arbitrary",)),
        name="ada",
    )(c, w_ada, b_ada.reshape(1, n))


def _carry_history(ext_ref, halo, tl, first_tile):
    @pl.when(first_tile)
    def _():
        ext_ref[:, 0:halo, :] = jnp.zeros((ext_ref.shape[0], halo, LANES), ext_ref.dtype)

    @pl.when(jnp.logical_not(first_tile))
    def _():
        ext_ref[:, 0:halo, :] = ext_ref[:, tl:tl + halo, :]


def _fill_slabs(ext_ref, halo, tl, value, first_slab=0, r0=0):
    rows = value.shape[0]
    for i in range(value.shape[1] // LANES):
        ext_ref[first_slab + i, halo + r0:halo + r0 + rows, :] = value[:, i * LANES:(i + 1) * LANES]


def _causal_conv_slab(ext_ref, slab, w_ref, b_ref, width, halo, r0, rows):
    c0 = slab * LANES
    base = r0 + halo - (width - 1)
    acc = jnp.broadcast_to(b_ref[:, c0:c0 + LANES], (rows, LANES))
    for k in range(width):
        acc = acc + w_ref[k:k + 1, c0:c0 + LANES] * ext_ref[slab, base + k:base + k + rows, :]
    return acc


HALO_A = 32
HALO_B = SUBLANES
CONV_ROWS = 64
MXU_TILE = 256
MXU_TILE_COST = 256


def _emit_interleaved(mxu_tasks, valu_tasks):
    t_m = t_v = 0
    vi = 0
    for mi, (fn, cost, epi, need_v) in enumerate(mxu_tasks):
        while vi < need_v:
            valu_tasks[vi][0]()
            t_v += valu_tasks[vi][1]
            vi += 1
        t_v = max(t_v, t_m)
        fn()
        t_m += cost
        t_v += epi
        while (vi < len(valu_tasks) and valu_tasks[vi][2] <= mi + 1
               and t_v + valu_tasks[vi][1] // 2 <= t_m):
            valu_tasks[vi][0]()
            t_v += valu_tasks[vi][1]
            vi += 1
    for fn, _, _ in valu_tasks[vi:]:
        fn()


def _front_kernel(x_ref, mod_ref, n1g_ref, w_ref, wg_ref, bg_ref, dtb_ref,
                  cwa_ref, cba_ref, lg_ref, lb_ref, wa_ref, ba_ref, cws_ref, cbs_ref,
                  oa_ref, h_ref, xc_ref, dt_ref,
                  exta_ref, extb_ref, conv_ref, s_ref, *, tl):
    first = pl.program_id(1) == 0
    _carry_history(exta_ref, HALO_A, tl, first)
    _carry_history(extb_ref, HALO_B, tl, first)

    x = x_ref[...]
    ms = jnp.mean(x * x, axis=-1, keepdims=True)
    y = x * lax.rsqrt(ms + EPS) * n1g_ref[...]
    h = (y * (1.0 + mod_ref[0, 1:2, :]) + mod_ref[0, 0:1, :]).astype(BF16)
    h_ref[...] = h

    tn = MXU_TILE
    mxu, valu = [], []

    def proj(c0, n=tn):
        return _dot(h, w_ref[:, c0:c0 + n])

    def glu_task(p):
        def fn():
            a_val = proj(C_AVAL + p * tn)
            a_gate = proj(C_AGATE + p * tn)
            _fill_slabs(exta_ref, HALO_A, tl, a_val * _sigmoid(a_gate), p * tn // LANES)
        return fn

    def xbc_task(t):
        def fn():
            _fill_slabs(extb_ref, HALO_B, tl, proj(C_XBC + t * tn), t * tn // LANES)
        return fn

    def outa_task(t):
        def fn():
            out_a = _dot(s_ref[...], wa_ref[:, t * tn:(t + 1) * tn]) + ba_ref[:, t * tn:(t + 1) * tn]
            oa_ref[:, t * tn:(t + 1) * tn] = oa_ref[:, t * tn:(t + 1) * tn] * out_a
        return fn

    def dt_task():
        dt_raw = proj(C_DT, DT_PAD) + dtb_ref[...]
        dt_ref[...] = jnp.maximum(dt_raw, 0.0) + jnp.log1p(jnp.exp(-jnp.abs(dt_raw)))

    def gate_task(t):
        def fn():
            oa_ref[:, t * tn:(t + 1) * tn] = _sigmoid(
                _dot(h, wg_ref[:, t * tn:(t + 1) * tn]) + bg_ref[:, t * tn:(t + 1) * tn])
        return fn

    def conv_a_task(sl):
        def fn():
            for r0 in range(0, tl, CONV_ROWS):
                conv_ref[r0:r0 + CONV_ROWS, sl * LANES:(sl + 1) * LANES] = _causal_conv_slab(
                    exta_ref, sl, cwa_ref, cba_ref, KERNEL_A, HALO_A, r0, CONV_ROWS)
        return fn

    def ln_task(r0):
        def fn():
            v = conv_ref[r0:r0 + CONV_ROWS, :]
            mu = jnp.mean(v, axis=-1, keepdims=True)
            vc = v - mu
            var = jnp.mean(vc * vc, axis=-1, keepdims=True)
            s_ref[r0:r0 + CONV_ROWS, :] = _silu(
                vc * lax.rsqrt(var + EPS) * lg_ref[...] + lb_ref[...]).astype(BF16)
        return fn

    def conv_b_task(sl):
        def fn():
            for r0 in range(0, tl, CONV_ROWS):
                xc_ref[r0:r0 + CONV_ROWS, sl * LANES:(sl + 1) * LANES] = _silu(_causal_conv_slab(
                    extb_ref, sl, cws_ref, cbs_ref, KERNEL_SSM, HALO_B, r0, CONV_ROWS)).astype(BF16)
        return fn

    n_glu = D_CONV_A // tn
    n_xbc = D_XBC // tn
    for p in range(n_glu):
        mxu.append((glu_task(p), 2 * MXU_TILE_COST, 80, 0))
        for sl in range(p * tn // LANES, (p + 1) * tn // LANES):
            valu.append((conv_a_task(sl), 500, len(mxu)))
    for r0 in range(0, tl, CONV_ROWS):
        valu.append((ln_task(r0), 200, len(mxu)))
    n_ln_done = len(valu)
    for t in range(n_xbc):
        mxu.append((xbc_task(t), MXU_TILE_COST, 0, 0))
        for sl in range(t * tn // LANES, (t + 1) * tn // LANES):
            valu.append((conv_b_task(sl), 115, len(mxu)))
    mxu.append((dt_task, MXU_TILE_COST // 2, 40, 0))
    for t in range(D_MODEL // tn):
        mxu.append((gate_task(t), MXU_TILE_COST, 80, 0))
    for t in range(D_MODEL // tn):
        mxu.append((outa_task(t), MXU_TILE_COST, 40, n_ln_done))
    _emit_interleaved(mxu, valu)


def _ssd_kernel(xc_ref, dt_ref, h_ref, oa_ref, x_ref, mod_ref,
                alog_ref, dsk_ref, ng_ref, wb_ref, wo_ref, wz_ref, wgb_ref, bgb_ref,
                o_ref, y_ref, s_ref, yn_ref, mg_ref, zs_ref, gb_ref, *, tl):
    @pl.when(pl.program_id(1) == 0)
    def _():
        s_ref[...] = jnp.zeros(s_ref.shape, s_ref.dtype)

    q = SSD_CHUNK
    a_neg = -jnp.exp(alog_ref[...])
    rows = lax.broadcasted_iota(jnp.int32, (q, q), 0)
    cols = lax.broadcasted_iota(jnp.int32, (q, q), 1)
    causal = rows >= cols
    tril_b = causal.astype(BF16)
    lane = lax.broadcasted_iota(jnp.int32, (q, LANES), 1)
    left = lane < HEAD_DIM

    def chunk_prep(ci):
        r0 = ci * q
        dtq = dt_ref[r0:r0 + q, :]
        da = dtq * a_neg
        da_hi = da.astype(BF16)
        rem = da - da_hi.astype(F32)
        da_mid = rem.astype(BF16)
        da_lo = (rem - da_mid.astype(F32)).astype(BF16)
        csum = _dot(tril_b, jnp.concatenate([da_hi, da_mid, da_lo], axis=1))
        acum = (csum[:, 0:DT_PAD] + csum[:, DT_PAD:2 * DT_PAD]) + csum[:, 2 * DT_PAD:3 * DT_PAD]
        col2 = acum * LOG2_E
        row2 = col2 - jnp.log2(dtq)
        return dict(col2=col2, row2=row2, row2_t=row2.T)

    def group_head(ci, g, ck):
        r0 = ci * q
        c0 = g * GROUP_W
        bg_t = xc_ref[r0:r0 + q, D_INNER + g * D_STATE:D_INNER + (g + 1) * D_STATE].T
        cg = xc_ref[r0:r0 + q, D_INNER + D_BC + g * D_STATE:D_INNER + D_BC + (g + 1) * D_STATE]
        cb_mat = _dot(cg, bg_t)
        y_off = _dot(cg, s_ref[:, c0:c0 + GROUP_W].astype(BF16))
        return cb_mat, y_off, bg_t

    def group_tail(ci, g, ck, cb_mat, y_off, bg_t):
        r0 = ci * q
        c0 = g * GROUP_W
        col2, row2, row2_t = ck["col2"], ck["row2"], ck["row2_t"]
        xw_parts, sdec_parts = [], []
        for pr in range(HEADS_PER_GROUP // 2):
            h1 = g * HEADS_PER_GROUP + 2 * pr
            lanes = slice(c0 + pr * LANES, c0 + (pr + 1) * LANES)

            def scores(h):
                seg2 = col2[:, h:h + 1] - row2_t[h:h + 1, :]
                return (jnp.where(causal, jnp.exp2(seg2), 0.0) * cb_mat).astype(BF16)

            col_p = jnp.where(left, col2[:, h1:h1 + 1], col2[:, h1 + 1:h1 + 2])
            row_p = jnp.where(left, row2[:, h1:h1 + 1], row2[:, h1 + 1:h1 + 2])
            last_p = col_p[q - 1:q, :]
            xp = xc_ref[r0:r0 + q, lanes]
            xw_parts.append((xp.astype(F32) * jnp.exp2(last_p - row_p)).astype(BF16))
            sdec_parts.append(jnp.exp2(last_p))

            lhs = jnp.concatenate([scores(h1), scores(h1 + 1)], axis=1)
            zero = jnp.zeros_like(xp)
            rhs = jnp.concatenate([jnp.where(left, xp, zero),
                                   jnp.where(left, zero, xp)], axis=0)
            y_ref[r0:r0 + q, lanes] = (
                _dot(lhs, rhs) + y_off[:, pr * LANES:(pr + 1) * LANES] * jnp.exp2(col_p))

        new_state = _dot(bg_t, jnp.concatenate(xw_parts, axis=1))
        s_ref[:, c0:c0 + GROUP_W] = (
            s_ref[:, c0:c0 + GROUP_W] * jnp.concatenate(sdec_parts, axis=1) + new_state)

    tn = MXU_TILE
    side = []
    for t in range(D_INNER // tn):
        def z_tile(t=t):
            zs_ref[:, t * tn:(t + 1) * tn] = _silu(_dot(h_ref[...], wz_ref[:, t * tn:(t + 1) * tn])).astype(BF16)
        side.append(z_tile)
    for t in range(D_MODEL // tn):
        def gb_tile(t=t):
            gb_ref[:, t * tn:(t + 1) * tn] = _sigmoid(
                _dot(h_ref[...], wgb_ref[:, t * tn:(t + 1) * tn]) + bgb_ref[:, t * tn:(t + 1) * tn]).astype(BF16)
        side.append(gb_tile)

    n_chunks = tl // q
    ck = chunk_prep(0)
    head = group_head(0, 0, ck)
    for ci in range(n_chunks):
        ck_next = None
        for g in range(N_GROUPS):
            cur = head
            if g + 1 < N_GROUPS:
                head = group_head(ci, g + 1, ck)
            elif ci + 1 < n_chunks:
                ck_next = chunk_prep(ci + 1)
            if side:
                side.pop(0)()
            group_tail(ci, g, ck, *cur)
        if ci + 1 < n_chunks:
            ck = ck_next
            head = group_head(ci + 1, 0, ck)

    while side:
        side.pop(0)()
    y = y_ref[...] + xc_ref[:, 0:D_INNER].astype(F32) * dsk_ref[...]
    y = y * zs_ref[...].astype(F32)
    for g in range(N_GROUPS):
        yg = y[:, g * GROUP_W:(g + 1) * GROUP_W]
        ms = jnp.mean(yg * yg, axis=-1, keepdims=True)
        yn_ref[:, g * GROUP_W:(g + 1) * GROUP_W] = (
            yg * lax.rsqrt(ms + EPS_SSM_NORM) * ng_ref[:, g * GROUP_W:(g + 1) * GROUP_W]).astype(BF16)
    out_b = _dot(yn_ref[...], wb_ref[...])
    mg_ref[...] = (oa_ref[...] + gb_ref[...].astype(F32) * out_b).astype(BF16)
    mix = _dot(mg_ref[...], wo_ref[...])
    o_ref[...] = x_ref[...] + mod_ref[0, 2:3, :] * mix


HALO_F = SUBLANES
COLS_F = 256
FFN_TILE = 512
FFN_HALVES = 2


def _ffn_kernel(x_ref, mod_ref, n2g_ref, wup_ref, cw_ref, cb_ref, wdn_ref, nfg_ref,
                o_ref, ext_ref, h2_ref, *, tl):
    _carry_history(ext_ref, HALO_F, tl, pl.program_id(1) == 0)
    hr = tl // FFN_HALVES

    def rows(hf):
        return slice(hf * hr, (hf + 1) * hr)

    def prologue(hf):
        x = x_ref[rows(hf), :]
        ms = jnp.mean(x * x, axis=-1, keepdims=True)
        y = x * lax.rsqrt(ms + EPS) * n2g_ref[...]
        h2_ref[rows(hf), :] = (y * (1.0 + mod_ref[0, 4:5, :]) + mod_ref[0, 3:4, :]).astype(BF16)

    def up_cols(hf, blk):
        for c0 in (blk * COLS_F, D_FF + blk * COLS_F):
            _fill_slabs(ext_ref, HALO_F, tl, _dot(h2_ref[rows(hf), :], wup_ref[:, c0:c0 + COLS_F]),
                        c0 // LANES, hf * hr)

    def act_cols(hf, blk):
        def conv(c0):
            return jnp.concatenate(
                [_causal_conv_slab(ext_ref, c0 // LANES + i, cw_ref, cb_ref, KERNEL_FFN, HALO_F, hf * hr, hr)
                 for i in range(COLS_F // LANES)], axis=1)
        return (_silu(conv(blk * COLS_F)) * conv(D_FF + blk * COLS_F)).astype(BF16)

    def epilogue(hf, down):
        xo = x_ref[rows(hf), :] + mod_ref[0, 5:6, :] * down
        ms2 = jnp.mean(xo * xo, axis=-1, keepdims=True)
        o_ref[rows(hf), :] = xo * lax.rsqrt(ms2 + EPS) * nfg_ref[...]

    nblk = D_FF // COLS_F
    units = [(hf, blk) for hf in range(FFN_HALVES) for blk in range(nblk)]
    down = [jnp.zeros((hr, D_MODEL), F32) for _ in range(FFN_HALVES)]
    prologue(0)
    up_cols(*units[0])
    for i, (hf, blk) in enumerate(units):
        if hf + 1 < FFN_HALVES and blk == nblk // 2:
            prologue(hf + 1)
        if i + 1 < len(units):
            up_cols(*units[i + 1])
        down[hf] = down[hf] + _dot(act_cols(hf, blk), wdn_ref[blk * COLS_F:(blk + 1) * COLS_F, :])
        if hf > 0 and blk == 1:
            epilogue(hf - 1, down[hf - 1])
    epilogue(FFN_HALVES - 1, down[FFN_HALVES - 1])


def kernel(x, c, w_ada, b_ada, norm1_g, w_in, conv_a_w, conv_a_b, ln_a_g, ln_a_b, w_a_out, b_a_out, conv_ssm_w, conv_ssm_b, dt_bias, a_log, d_skip, ssm_norm_g, w_b_out, w_gate, b_gate, w_o, norm2_g, w_up, conv_ffn_w, conv_ffn_b, w_down, norm_f_g):
    bsz, seqlen, d = x.shape
    assert d == D_MODEL and w_ada.shape[0] == 1
    tl = SEQ_TILE
    assert seqlen % SEQ_TILE == 0 and seqlen % SSD_TILE == 0 and seqlen % FFN_TILE == 0
    assert SSD_TILE % SSD_CHUNK == 0 and FFN_TILE % FFN_HALVES == 0

    row = lambda v: v.reshape(1, -1)
    n_dt = N_HEADS
    w_in_p = jnp.pad(w_in[0], ((0, 0), (0, DT_PAD - n_dt))).astype(BF16)
    dt_bias_pad = jnp.pad(dt_bias[0], (0, DT_PAD - n_dt)).reshape(1, DT_PAD)
    a_log_pad = jnp.pad(a_log[0], (0, DT_PAD - n_dt)).reshape(1, DT_PAD)
    d_skip_x = jnp.repeat(d_skip[0], HEAD_DIM).reshape(1, D_INNER)

    x2d = x.reshape(bsz * seqlen, D_MODEL)
    mod3 = _ada(c, w_ada[0], b_ada[0]).reshape(bsz, N_MOD, D_MODEL)

    oa, hb, xc, dt = _seq_call(
        _front_kernel, "front", bsz, seqlen,
        [(x2d, D_MODEL, 0), (mod3, None, None)],
        [row(norm1_g[0]), w_in_p, w_gate[0][:, :D_MODEL].astype(BF16), row(b_gate[0][:D_MODEL]), dt_bias_pad,
         conv_a_w[0], row(conv_a_b[0]), row(ln_a_g[0]), row(ln_a_b[0]),
         w_a_out[0].astype(BF16), row(b_a_out[0]), conv_ssm_w[0], row(conv_ssm_b[0])],
        [(D_MODEL, F32), (D_MODEL, BF16), (D_XBC, BF16), (DT_PAD, F32)],
        [pltpu.VMEM((D_CONV_A // LANES, HALO_A + tl, LANES), F32),
         pltpu.VMEM((D_XBC // LANES, HALO_B + tl, LANES), F32),
         pltpu.VMEM((tl, D_CONV_A), F32),
         pltpu.VMEM((tl, D_CONV_A), BF16)])

    (x1,) = _seq_call(
        _ssd_kernel, "ssd", bsz, seqlen,
        [(xc, D_XBC, 0), (dt, DT_PAD, 0), (hb, D_MODEL, 0), (oa, D_MODEL, 0),
         (x2d, D_MODEL, 0), (mod3, None, None)],
        [a_log_pad, d_skip_x, row(ssm_norm_g[0]),
         w_b_out[0].astype(BF16), w_o[0].astype(BF16),
         w_in[0][:, C_Z:C_XBC].astype(BF16), w_gate[0][:, D_MODEL:].astype(BF16), row(b_gate[0][D_MODEL:])],
        [(D_MODEL, F32)],
        [pltpu.VMEM((SSD_TILE, D_INNER), F32), pltpu.VMEM((D_STATE, D_INNER), F32),
         pltpu.VMEM((SSD_TILE, D_INNER), BF16), pltpu.VMEM((SSD_TILE, D_MODEL), BF16),
         pltpu.VMEM((SSD_TILE, D_INNER), BF16), pltpu.VMEM((SSD_TILE, D_MODEL), BF16)], tl=SSD_TILE)

    (out,) = _seq_call(
        _ffn_kernel, "ffn", bsz, seqlen,
        [(x1, D_MODEL, 0), (mod3, None, None)],
        [row(norm2_g[0]), w_up[0].astype(BF16), conv_ffn_w[0], row(conv_ffn_b[0]),
         w_down[0].astype(BF16), row(norm_f_g)],
        [(D_MODEL, F32)],
        [pltpu.VMEM((2 * D_FF // LANES, HALO_F + FFN_TILE, LANES), F32),
         pltpu.VMEM((FFN_TILE, D_MODEL), BF16)], tl=FFN_TILE)
    return out.reshape(bsz, seqlen, D_MODEL)
```

```python
import functools

import jax
import jax.numpy as jnp
from jax import lax
from jax.experimental import pallas as pl
from jax.experimental.pallas import tpu as pltpu

F32 = jnp.float32
BF16 = jnp.bfloat16

D_MODEL = 1024
D_CONV_A = D_MODEL
KERNEL_A = 31
D_INNER = 2 * D_MODEL
HEAD_DIM = 64
N_HEADS = D_INNER // HEAD_DIM
N_GROUPS = 4
HEADS_PER_GROUP = N_HEADS // N_GROUPS
D_STATE = 128
KERNEL_SSM = 4
D_BC = N_GROUPS * D_STATE
D_XBC = D_INNER + 2 * D_BC
D_FF = 2816
KERNEL_FFN = 3
N_MOD = 6
EPS = 1e-6
EPS_SSM_NORM = 1e-5
LOG2_E = 1.4426950408889634

LANES = 128
SUBLANES = 8
DT_PAD = LANES
SSD_CHUNK = 128
GROUP_W = HEADS_PER_GROUP * HEAD_DIM

C_AVAL = 0
C_AGATE = C_AVAL + D_CONV_A
C_Z = C_AGATE + D_CONV_A
C_XBC = C_Z + D_INNER
C_DT = C_XBC + D_XBC

VMEM_LIMIT = 56 * 1024 * 1024
SEQ_TILE = 256
SSD_TILE = 512


def _dot(a, b):
    return jnp.dot(a, b, preferred_element_type=F32)


def _sigmoid(x):
    return jax.nn.sigmoid(x)


def _silu(x):
    return x * _sigmoid(x)


def _resident(shape):
    nd = len(shape)
    return pl.BlockSpec(shape, lambda *_: (0,) * nd, pipeline_mode=pl.Buffered(1))


def _seq_call(body, name, bsz, seqlen, tiled_in, resident_in, outs, scratch, tl=SEQ_TILE):
    per_seq = seqlen // tl
    in_specs, args = [], []
    for arr, cols, cidx in tiled_in:
        if cols is None:
            in_specs.append(pl.BlockSpec((1,) + arr.shape[1:], lambda b, j: (b, 0, 0)))
        else:
            in_specs.append(pl.BlockSpec((tl, cols), lambda b, j, c=cidx: (b * per_seq + j, c)))
        args.append(arr)
    for arr in resident_in:
        in_specs.append(_resident(arr.shape))
        args.append(arr)
    row = lambda b, j: (b * per_seq + j, 0)
    return pl.pallas_call(
        functools.partial(body, tl=tl),
        grid=(bsz, per_seq),
        in_specs=in_specs,
        out_specs=[pl.BlockSpec((tl, cols), row) for cols, _ in outs],
        out_shape=[jax.ShapeDtypeStruct((bsz * seqlen, cols), dt) for cols, dt in outs],
        scratch_shapes=scratch,
        compiler_params=pltpu.CompilerParams(
            dimension_semantics=("arbitrary", "arbitrary"), vmem_limit_bytes=VMEM_LIMIT),
        name=name,
    )(*args)


def _ada_kernel(c_ref, w_ref, b_ref, o_ref):
    s = _silu(c_ref[...]).astype(BF16)
    o_ref[...] = _dot(s, w_ref[...].astype(BF16)) + b_ref[...]


def _ada(c, w_ada, b_ada):
    bsz = c.shape[0]
    n = w_ada.shape[1]
    tn = D_MODEL
    return pl.pallas_call(
        _ada_kernel,
        grid=(n // tn,),
        in_specs=[
            pl.BlockSpec((bsz, D_MODEL), lambda j: (0, 0)),
            pl.BlockSpec((D_MODEL, tn), lambda j: (0, j)),
            pl.BlockSpec((1, tn), lambda j: (0, j)),
        ],
        out_specs=pl.BlockSpec((bsz, tn), lambda j: (0, j)),
        out_shape=jax.ShapeDtypeStruct((bsz, n), F32),
        compiler_params=pltpu.CompilerParams(dimension_semantics=("arbitrary",)),
        name="ada",
    )(c, w_ada, b_ada.reshape(1, n))


def _carry_history(ext_ref, halo, tl, first_tile):
    @pl.when(first_tile)
    def _():
        ext_ref[:, 0:halo, :] = jnp.zeros((ext_ref.shape[0], halo, LANES), ext_ref.dtype)

    @pl.when(jnp.logical_not(first_tile))
    def _():
        ext_ref[:, 0:halo, :] = ext_ref[:, tl:tl + halo, :]


def _fill_slabs(ext_ref, halo, tl, value, first_slab=0, r0=0):
    rows = value.shape[0]
    for i in range(value.shape[1] // LANES):
        ext_ref[first_slab + i, halo + r0:halo + r0 + rows, :] = value[:, i * LANES:(i + 1) * LANES]


def _causal_conv_slab(ext_ref, slab, w_ref, b_ref, width, halo, r0, rows):
    c0 = slab * LANES
    base = r0 + halo - (width - 1)
    acc = jnp.broadcast_to(b_ref[:, c0:c0 + LANES], (rows, LANES))
    for k in range(width):
        acc = acc + w_ref[k:k + 1, c0:c0 + LANES] * ext_ref[slab, base + k:base + k + rows, :]
    return acc


HALO_A = 32
HALO_B = SUBLANES
CONV_ROWS = 64
MXU_TILE = 256
MXU_TILE_COST = 256
GLU_SPLIT = 2


def _emit_interleaved(mxu_tasks, valu_tasks):
    t_m = t_v = 0
    vi = 0
    for mi, (fn, cost, epi, need_v) in enumerate(mxu_tasks):
        while vi < need_v:
            valu_tasks[vi][0]()
            t_v += valu_tasks[vi][1]
            vi += 1
        t_v = max(t_v, t_m)
        fn()
        t_m += cost
        t_v += epi
        while (vi < len(valu_tasks) and valu_tasks[vi][2] <= mi + 1
               and t_v + valu_tasks[vi][1] // 2 <= t_m):
            valu_tasks[vi][0]()
            t_v += valu_tasks[vi][1]
            vi += 1
    for fn, _, _ in valu_tasks[vi:]:
        fn()


def _front_kernel(x_ref, mod_ref, n1g_ref, w_ref, wg_ref, bg_ref, dtb_ref,
                  cwa_ref, cba_ref, lg_ref, lb_ref, wa_ref, ba_ref,
                  oa_ref, h_ref, xr_ref, dt_ref,
                  exta_ref, conv_ref, s_ref, *, tl):
    first = pl.program_id(1) == 0
    _carry_history(exta_ref, HALO_A, tl, first)

    x = x_ref[...]
    ms = jnp.mean(x * x, axis=-1, keepdims=True)
    y = x * lax.rsqrt(ms + EPS) * n1g_ref[...]
    h = (y * (1.0 + mod_ref[0, 1:2, :]) + mod_ref[0, 0:1, :]).astype(BF16)
    h_ref[...] = h

    tn = MXU_TILE
    mxu, valu = [], []

    def proj(c0, n=tn):
        return _dot(h, w_ref[:, c0:c0 + n])

    a_vals = {}
    gr = tl // GLU_SPLIT

    def aval_task(p):
        def fn():
            a_vals[p] = proj(C_AVAL + p * tn)
        return fn

    def glu_task(p, piece):
        def fn():
            r0 = piece * gr
            a_gate = _dot(h[r0:r0 + gr, :], w_ref[:, C_AGATE + p * tn:C_AGATE + (p + 1) * tn])
            _fill_slabs(exta_ref, HALO_A, tl, a_vals[p][r0:r0 + gr, :] * _sigmoid(a_gate),
                        p * tn // LANES, r0)
        return fn

    def xbc_task(t):
        def fn():
            xr_ref[:, t * tn:(t + 1) * tn] = proj(C_XBC + t * tn).astype(BF16)
        return fn

    def outa_task(t):
        def fn():
            out_a = _dot(s_ref[...], wa_ref[:, t * tn:(t + 1) * tn]) + ba_ref[:, t * tn:(t + 1) * tn]
            oa_ref[:, t * tn:(t + 1) * tn] = oa_ref[:, t * tn:(t + 1) * tn] * out_a
        return fn

    def dt_task():
        dt_raw = proj(C_DT, DT_PAD) + dtb_ref[...]
        dt_ref[...] = jnp.maximum(dt_raw, 0.0) + jnp.log1p(jnp.exp(-jnp.abs(dt_raw)))

    def gate_task(t):
        def fn():
            oa_ref[:, t * tn:(t + 1) * tn] = _sigmoid(
                _dot(h, wg_ref[:, t * tn:(t + 1) * tn]) + bg_ref[:, t * tn:(t + 1) * tn])
        return fn

    def conv_a_task(sl, piece):
        def fn():
            for r0 in range(piece * gr, (piece + 1) * gr, CONV_ROWS):
                conv_ref[r0:r0 + CONV_ROWS, sl * LANES:(sl + 1) * LANES] = _causal_conv_slab(
                    exta_ref, sl, cwa_ref, cba_ref, KERNEL_A, HALO_A, r0, CONV_ROWS)
        return fn

    def ln_task(r0):
        def fn():
            v = conv_ref[r0:r0 + CONV_ROWS, :]
            mu = jnp.mean(v, axis=-1, keepdims=True)
            vc = v - mu
            var = jnp.mean(vc * vc, axis=-1, keepdims=True)
            s_ref[r0:r0 + CONV_ROWS, :] = _silu(
                vc * lax.rsqrt(var + EPS) * lg_ref[...] + lb_ref[...]).astype(BF16)
        return fn

    n_glu = D_CONV_A // tn
    xbc_tiles = list(range(D_XBC // tn))
    for p in range(n_glu):
        mxu.append((aval_task(p), MXU_TILE_COST, 0, 0))
    for p in range(n_glu):
        for piece in range(GLU_SPLIT):
            mxu.append((glu_task(p, piece), MXU_TILE_COST // GLU_SPLIT, 40, 0))
            for sl in range(p * tn // LANES, (p + 1) * tn // LANES):
                valu.append((conv_a_task(sl, piece), 250, len(mxu)))
            for _ in range(1 + piece):
                if xbc_tiles:
                    mxu.append((xbc_task(xbc_tiles.pop(0)), MXU_TILE_COST, 20, 0))
    for r0 in range(0, tl, CONV_ROWS):
        valu.append((ln_task(r0), 200, len(mxu)))
    n_ln_done = len(valu)
    while xbc_tiles:
        mxu.append((xbc_task(xbc_tiles.pop(0)), MXU_TILE_COST, 20, 0))
    mxu.append((dt_task, MXU_TILE_COST // 2, 40, 0))
    for t in range(D_MODEL // tn):
        mxu.append((gate_task(t), MXU_TILE_COST, 80, 0))
    for t in range(D_MODEL // tn):
        mxu.append((outa_task(t), MXU_TILE_COST, 40, n_ln_done))
    _emit_interleaved(mxu, valu)


def _ssd_kernel(xr_ref, dt_ref, h_ref, oa_ref, x_ref, mod_ref,
                alog_ref, dsk_ref, ng_ref, wb_ref, wo_ref, wz_ref, wgb_ref, bgb_ref, cws_ref, cbs_ref,
                o_ref, y_ref, s_ref, yn_ref, mg_ref, zs_ref, gb_ref, extb_ref, xc_ref, *, tl):
    first = pl.program_id(1) == 0

    @pl.when(first)
    def _():
        s_ref[...] = jnp.zeros(s_ref.shape, s_ref.dtype)

    _carry_history(extb_ref, HALO_B, tl, first)
    _fill_slabs(extb_ref, HALO_B, tl, xr_ref[...].astype(F32))
    for sl in range(D_XBC // LANES):
        for r0 in range(0, tl, CONV_ROWS):
            xc_ref[r0:r0 + CONV_ROWS, sl * LANES:(sl + 1) * LANES] = _silu(_causal_conv_slab(
                extb_ref, sl, cws_ref, cbs_ref, KERNEL_SSM, HALO_B, r0, CONV_ROWS)).astype(BF16)

    q = SSD_CHUNK
    a_neg = -jnp.exp(alog_ref[...])
    rows = lax.broadcasted_iota(jnp.int32, (q, q), 0)
    cols = lax.broadcasted_iota(jnp.int32, (q, q), 1)
    causal = rows >= cols
    tril_b = causal.astype(BF16)
    lane = lax.broadcasted_iota(jnp.int32, (q, LANES), 1)
    left = lane < HEAD_DIM

    def chunk_prep(ci):
        r0 = ci * q
        dtq = dt_ref[r0:r0 + q, :]
        da = dtq * a_neg
        da_hi = da.astype(BF16)
        rem = da - da_hi.astype(F32)
        da_mid = rem.astype(BF16)
        da_lo = (rem - da_mid.astype(F32)).astype(BF16)
        csum = _dot(tril_b, jnp.concatenate([da_hi, da_mid, da_lo], axis=1))
        acum = (csum[:, 0:DT_PAD] + csum[:, DT_PAD:2 * DT_PAD]) + csum[:, 2 * DT_PAD:3 * DT_PAD]
        col2 = acum * LOG2_E
        row2 = col2 - jnp.log2(dtq)
        return dict(col2=col2, row2=row2, row2_t=row2.T)

    def group_head(ci, g, ck):
        r0 = ci * q
        c0 = g * GROUP_W
        bg_t = xc_ref[r0:r0 + q, D_INNER + g * D_STATE:D_INNER + (g + 1) * D_STATE].T
        cg = xc_ref[r0:r0 + q, D_INNER + D_BC + g * D_STATE:D_INNER + D_BC + (g + 1) * D_STATE]
        cb_mat = _dot(cg, bg_t)
        y_off = _dot(cg, s_ref[:, c0:c0 + GROUP_W].astype(BF16))
        return cb_mat, y_off, bg_t

    def group_tail(ci, g, ck, cb_mat, y_off, bg_t):
        r0 = ci * q
        c0 = g * GROUP_W
        col2, row2, row2_t = ck["col2"], ck["row2"], ck["row2_t"]
        xw_parts, sdec_parts = [], []
        for pr in range(HEADS_PER_GROUP // 2):
            h1 = g * HEADS_PER_GROUP + 2 * pr
            lanes = slice(c0 + pr * LANES, c0 + (pr + 1) * LANES)

            def scores(h):
                seg2 = col2[:, h:h + 1] - row2_t[h:h + 1, :]
                return (jnp.where(causal, jnp.exp2(seg2), 0.0) * cb_mat).astype(BF16)

            col_p = jnp.where(left, col2[:, h1:h1 + 1], col2[:, h1 + 1:h1 + 2])
            row_p = jnp.where(left, row2[:, h1:h1 + 1], row2[:, h1 + 1:h1 + 2])
            last_p = col_p[q - 1:q, :]
            xp = xc_ref[r0:r0 + q, lanes]
            xw_parts.append((xp.astype(F32) * jnp.exp2(last_p - row_p)).astype(BF16))
            sdec_parts.append(jnp.exp2(last_p))

            lhs = jnp.concatenate([scores(h1), scores(h1 + 1)], axis=1)
            zero = jnp.zeros_like(xp)
            rhs = jnp.concatenate([jnp.where(left, xp, zero),
                                   jnp.where(left, zero, xp)], axis=0)
            y_ref[r0:r0 + q, lanes] = (
                _dot(lhs, rhs) + y_off[:, pr * LANES:(pr + 1) * LANES] * jnp.exp2(col_p))

        new_state = _dot(bg_t, jnp.concatenate(xw_parts, axis=1))
        s_ref[:, c0:c0 + GROUP_W] = (
            s_ref[:, c0:c0 + GROUP_W] * jnp.concatenate(sdec_parts, axis=1) + new_state)

    tn = MXU_TILE
    side = []
    for t in range(D_INNER // tn):
        def z_tile(t=t):
            zs_ref[:, t * tn:(t + 1) * tn] = _silu(_dot(h_ref[...], wz_ref[:, t * tn:(t + 1) * tn])).astype(BF16)
        side.append(z_tile)
    for t in range(D_MODEL // tn):
        def gb_tile(t=t):
            gb_ref[:, t * tn:(t + 1) * tn] = _sigmoid(
                _dot(h_ref[...], wgb_ref[:, t * tn:(t + 1) * tn]) + bgb_ref[:, t * tn:(t + 1) * tn]).astype(BF16)
        side.append(gb_tile)

    n_chunks = tl // q
    ck = chunk_prep(0)
    head = group_head(0, 0, ck)
    for ci in range(n_chunks):
        ck_next = None
        for g in range(N_GROUPS):
            cur = head
            if g + 1 < N_GROUPS:
                head = group_head(ci, g + 1, ck)
            elif ci + 1 < n_chunks:
                ck_next = chunk_prep(ci + 1)
            if side:
                side.pop(0)()
            group_tail(ci, g, ck, *cur)
        if ci + 1 < n_chunks:
            ck = ck_next
            head = group_head(ci + 1, 0, ck)

    while side:
        side.pop(0)()
    y = y_ref[...] + xc_ref[:, 0:D_INNER].astype(F32) * dsk_ref[...]
    y = y * zs_ref[...].astype(F32)
    for g in range(N_GROUPS):
        yg = y[:, g * GROUP_W:(g + 1) * GROUP_W]
        ms = jnp.mean(yg * yg, axis=-1, keepdims=True)
        yn_ref[:, g * GROUP_W:(g + 1) * GROUP_W] = (
            yg * lax.rsqrt(ms + EPS_SSM_NORM) * ng_ref[:, g * GROUP_W:(g + 1) * GROUP_W]).astype(BF16)
    out_b = _dot(yn_ref[...], wb_ref[...])
    mg_ref[...] = (oa_ref[...] + gb_ref[...].astype(F32) * out_b).astype(BF16)
    mix = _dot(mg_ref[...], wo_ref[...])
    o_ref[...] = x_ref[...] + mod_ref[0, 2:3, :] * mix


HALO_F = SUBLANES
COLS_F = 256
FFN_TILE = 512
FFN_HALVES = 2


def _ffn_kernel(x_ref, mod_ref, n2g_ref, wup_ref, cw_ref, cb_ref, wdn_ref, nfg_ref,
                o_ref, ext_ref, h2_ref, *, tl):
    _carry_history(ext_ref, HALO_F, tl, pl.program_id(1) == 0)
    hr = tl // FFN_HALVES

    def rows(hf):
        return slice(hf * hr, (hf + 1) * hr)

    def prologue(hf):
        x = x_ref[rows(hf), :]
        ms = jnp.mean(x * x, axis=-1, keepdims=True)
        y = x * lax.rsqrt(ms + EPS) * n2g_ref[...]
        h2_ref[rows(hf), :] = (y * (1.0 + mod_ref[0, 4:5, :]) + mod_ref[0, 3:4, :]).astype(BF16)

    def up_cols(hf, blk):
        for c0 in (blk * COLS_F, D_FF + blk * COLS_F):
            _fill_slabs(ext_ref, HALO_F, tl, _dot(h2_ref[rows(hf), :], wup_ref[:, c0:c0 + COLS_F]),
                        c0 // LANES, hf * hr)

    def act_cols(hf, blk):
        def conv(c0):
            return jnp.concatenate(
                [_causal_conv_slab(ext_ref, c0 // LANES + i, cw_ref, cb_ref, KERNEL_FFN, HALO_F, hf * hr, hr)
                 for i in range(COLS_F // LANES)], axis=1)
        return (_silu(conv(blk * COLS_F)) * conv(D_FF + blk * COLS_F)).astype(BF16)

    def epilogue(hf, down):
        xo = x_ref[rows(hf), :] + mod_ref[0, 5:6, :] * down
        ms2 = jnp.mean(xo * xo, axis=-1, keepdims=True)
        o_ref[rows(hf), :] = xo * lax.rsqrt(ms2 + EPS) * nfg_ref[...]

    nblk = D_FF // COLS_F
    units = [(hf, blk) for hf in range(FFN_HALVES) for blk in range(nblk)]
    down = [jnp.zeros((hr, D_MODEL), F32) for _ in range(FFN_HALVES)]
    prologue(0)
    up_cols(*units[0])
    for i, (hf, blk) in enumerate(units):
        if hf + 1 < FFN_HALVES and blk == nblk // 2:
            prologue(hf + 1)
        if i + 1 < len(units):
            up_cols(*units[i + 1])
        down[hf] = down[hf] + _dot(act_cols(hf, blk), wdn_ref[blk * COLS_F:(blk + 1) * COLS_F, :])
        if hf > 0 and blk == 1:
            epilogue(hf - 1, down[hf - 1])
    epilogue(FFN_HALVES - 1, down[FFN_HALVES - 1])


def kernel(x, c, w_ada, b_ada, norm1_g, w_in, conv_a_w, conv_a_b, ln_a_g, ln_a_b, w_a_out, b_a_out, conv_ssm_w, conv_ssm_b, dt_bias, a_log, d_skip, ssm_norm_g, w_b_out, w_gate, b_gate, w_o, norm2_g, w_up, conv_ffn_w, conv_ffn_b, w_down, norm_f_g):
    bsz, seqlen, d = x.shape
    assert d == D_MODEL and w_ada.shape[0] == 1
    tl = SEQ_TILE
    assert seqlen % SEQ_TILE == 0 and seqlen % SSD_TILE == 0 and seqlen % FFN_TILE == 0
    assert SSD_TILE % SSD_CHUNK == 0 and FFN_TILE % FFN_HALVES == 0

    row = lambda v: v.reshape(1, -1)
    n_dt = N_HEADS
    w_in_p = jnp.pad(w_in[0], ((0, 0), (0, DT_PAD - n_dt))).astype(BF16)
    dt_bias_pad = jnp.pad(dt_bias[0], (0, DT_PAD - n_dt)).reshape(1, DT_PAD)
    a_log_pad = jnp.pad(a_log[0], (0, DT_PAD - n_dt)).reshape(1, DT_PAD)
    d_skip_x = jnp.repeat(d_skip[0], HEAD_DIM).reshape(1, D_INNER)

    x2d = x.reshape(bsz * seqlen, D_MODEL)
    mod3 = _ada(c, w_ada[0], b_ada[0]).reshape(bsz, N_MOD, D_MODEL)

    oa, hb, xr, dt = _seq_call(
        _front_kernel, "front", bsz, seqlen,
        [(x2d, D_MODEL, 0), (mod3, None, None)],
        [row(norm1_g[0]), w_in_p, w_gate[0][:, :D_MODEL].astype(BF16), row(b_gate[0][:D_MODEL]), dt_bias_pad,
         conv_a_w[0], row(conv_a_b[0]), row(ln_a_g[0]), row(ln_a_b[0]),
         w_a_out[0].astype(BF16), row(b_a_out[0])],
        [(D_MODEL, F32), (D_MODEL, BF16), (D_XBC, BF16), (DT_PAD, F32)],
        [pltpu.VMEM((D_CONV_A // LANES, HALO_A + tl, LANES), F32),
         pltpu.VMEM((tl, D_CONV_A), F32),
         pltpu.VMEM((tl, D_CONV_A), BF16)])

    (x1,) = _seq_call(
        _ssd_kernel, "ssd", bsz, seqlen,
        [(xr, D_XBC, 0), (dt, DT_PAD, 0), (hb, D_MODEL, 0), (oa, D_MODEL, 0),
         (x2d, D_MODEL, 0), (mod3, None, None)],
        [a_log_pad, d_skip_x, row(ssm_norm_g[0]),
         w_b_out[0].astype(BF16), w_o[0].astype(BF16),
         w_in[0][:, C_Z:C_XBC].astype(BF16), w_gate[0][:, D_MODEL:].astype(BF16), row(b_gate[0][D_MODEL:]),
         conv_ssm_w[0], row(conv_ssm_b[0])],
        [(D_MODEL, F32)],
        [pltpu.VMEM((SSD_TILE, D_INNER), F32), pltpu.VMEM((D_STATE, D_INNER), F32),
         pltpu.VMEM((SSD_TILE, D_INNER), BF16), pltpu.VMEM((SSD_TILE, D_MODEL), BF16),
         pltpu.VMEM((SSD_TILE, D_INNER), BF16), pltpu.VMEM((SSD_TILE, D_MODEL), BF16),
         pltpu.VMEM((D_XBC // LANES, HALO_B + SSD_TILE, LANES), F32), pltpu.VMEM((SSD_TILE, D_XBC), BF16)],
        tl=SSD_TILE)

    (out,) = _seq_call(
        _ffn_kernel, "ffn", bsz, seqlen,
        [(x1, D_MODEL, 0), (mod3, None, None)],
        [row(norm2_g[0]), w_up[0].astype(BF16), conv_ffn_w[0], row(conv_ffn_b[0]),
         w_down[0].astype(BF16), row(norm_f_g)],
        [(D_MODEL, F32)],
        [pltpu.VMEM((2 * D_FF // LANES, HALO_F + FFN_TILE, LANES), F32),
         pltpu.VMEM((FFN_TILE, D_MODEL), BF16)], tl=FFN_TILE)
    return out.reshape(bsz, seqlen, D_MODEL)
```

```python
import functools

import jax
import jax.numpy as jnp
from jax import lax
from jax.experimental import pallas as pl
from jax.experimental.pallas import tpu as pltpu

F32 = jnp.float32
BF16 = jnp.bfloat16

D_MODEL = 1024
D_CONV_A = D_MODEL
KERNEL_A = 31
D_INNER = 2 * D_MODEL
HEAD_DIM = 64
N_HEADS = D_INNER // HEAD_DIM
N_GROUPS = 4
HEADS_PER_GROUP = N_HEADS // N_GROUPS
D_STATE = 128
KERNEL_SSM = 4
D_BC = N_GROUPS * D_STATE
D_XBC = D_INNER + 2 * D_BC
D_FF = 2816
KERNEL_FFN = 3
N_MOD = 6
EPS = 1e-6
EPS_SSM_NORM = 1e-5
LOG2_E = 1.4426950408889634

LANES = 128
SUBLANES = 8
DT_PAD = LANES
SSD_CHUNK = 128
GROUP_W = HEADS_PER_GROUP * HEAD_DIM

C_AVAL = 0
C_AGATE = C_AVAL + D_CONV_A
C_Z = C_AGATE + D_CONV_A
C_XBC = C_Z + D_INNER
C_DT = C_XBC + D_XBC

VMEM_LIMIT = 56 * 1024 * 1024
SEQ_TILE = 256
SSD_TILE = 512


def _dot(a, b):
    return jnp.dot(a, b, preferred_element_type=F32)


def _sigmoid(x):
    return jax.nn.sigmoid(x)


def _silu(x):
    return x * _sigmoid(x)


def _resident(shape):
    nd = len(shape)
    return pl.BlockSpec(shape, lambda *_: (0,) * nd, pipeline_mode=pl.Buffered(1))


def _seq_call(body, name, bsz, seqlen, tiled_in, resident_in, outs, scratch, tl=SEQ_TILE):
    per_seq = seqlen // tl
    in_specs, args = [], []
    for arr, cols, cidx in tiled_in:
        if cols is None:
            in_specs.append(pl.BlockSpec((1,) + arr.shape[1:], lambda b, j: (b, 0, 0)))
        else:
            in_specs.append(pl.BlockSpec((tl, cols), lambda b, j, c=cidx: (b * per_seq + j, c)))
        args.append(arr)
    for arr in resident_in:
        in_specs.append(_resident(arr.shape))
        args.append(arr)
    row = lambda b, j: (b * per_seq + j, 0)
    return pl.pallas_call(
        functools.partial(body, tl=tl),
        grid=(bsz, per_seq),
        in_specs=in_specs,
        out_specs=[pl.BlockSpec((tl, cols), row) for cols, _ in outs],
        out_shape=[jax.ShapeDtypeStruct((bsz * seqlen, cols), dt) for cols, dt in outs],
        scratch_shapes=scratch,
        compiler_params=pltpu.CompilerParams(
            dimension_semantics=("arbitrary", "arbitrary"), vmem_limit_bytes=VMEM_LIMIT),
        name=name,
    )(*args)


def _ada_kernel(c_ref, w_ref, b_ref, o_ref):
    s = _silu(c_ref[...]).astype(BF16)
    o_ref[...] = _dot(s, w_ref[...].astype(BF16)) + b_ref[...]


def _ada(c, w_ada, b_ada):
    bsz = c.shape[0]
    n = w_ada.shape[1]
    tn = D_MODEL
    return pl.pallas_call(
        _ada_kernel,
        grid=(n // tn,),
        in_specs=[
            pl.BlockSpec((bsz, D_MODEL), lambda j: (0, 0)),
            pl.BlockSpec((D_MODEL, tn), lambda j: (0, j)),
            pl.BlockSpec((1, tn), lambda j: (0, j)),
        ],
        out_specs=pl.BlockSpec((bsz, tn), lambda j: (0, j)),
        out_shape=jax.ShapeDtypeStruct((bsz, n), F32),
        compiler_params=pltpu.CompilerParams(dimension_semantics=("arbitrary",)),
        name="ada",
    )(c, w_ada, b_ada.reshape(1, n))


def _carry_history(ext_ref, halo, tl, first_tile):
    @pl.when(first_tile)
    def _():
        ext_ref[:, 0:halo, :] = jnp.zeros((ext_ref.shape[0], halo, LANES), ext_ref.dtype)

    @pl.when(jnp.logical_not(first_tile))
    def _():
        ext_ref[:, 0:halo, :] = ext_ref[:, tl:tl + halo, :]


def _fill_slabs(ext_ref, halo, tl, value, first_slab=0, r0=0):
    rows = value.shape[0]
    for i in range(value.shape[1] // LANES):
        ext_ref[first_slab + i, halo + r0:halo + r0 + rows, :] = value[:, i * LANES:(i + 1) * LANES]


def _causal_conv_slab(ext_ref, slab, w_ref, b_ref, width, halo, r0, rows):
    c0 = slab * LANES
    base = r0 + halo - (width - 1)
    acc = jnp.broadcast_to(b_ref[:, c0:c0 + LANES], (rows, LANES))
    for k0 in range(0, width, TAP_SHARE * SUBLANES):
        for s in range(min(SUBLANES, width - k0)):
            taps = list(range(k0 + s, min(width, k0 + TAP_SHARE * SUBLANES), SUBLANES))
            span = rows + SUBLANES * (len(taps) - 1)
            window = ext_ref[slab, base + taps[0]:base + taps[0] + span, :]
            for i, k in enumerate(taps):
                acc = acc + w_ref[k:k + 1, c0:c0 + LANES] * window[SUBLANES * i:SUBLANES * i + rows, :]
    return acc


HALO_A = 32
HALO_B = SUBLANES
CONV_ROWS = 64
TAP_SHARE = 2
MXU_TILE = 256
MXU_TILE_COST = 256


def _emit_interleaved(mxu_tasks, valu_tasks):
    t_m = t_v = 0
    vi = 0
    for mi, (fn, cost, epi, need_v) in enumerate(mxu_tasks):
        while vi < need_v:
            valu_tasks[vi][0]()
            t_v += valu_tasks[vi][1]
            vi += 1
        t_v = max(t_v, t_m)
        fn()
        t_m += cost
        t_v += epi
        while (vi < len(valu_tasks) and valu_tasks[vi][2] <= mi + 1
               and t_v + valu_tasks[vi][1] // 2 <= t_m):
            valu_tasks[vi][0]()
            t_v += valu_tasks[vi][1]
            vi += 1
    for fn, _, _ in valu_tasks[vi:]:
        fn()


def _front_kernel(x_ref, mod_ref, n1g_ref, w_ref, wg_ref, bg_ref, dtb_ref,
                  cwa_ref, cba_ref, lg_ref, lb_ref, wa_ref, ba_ref, cws_ref, cbs_ref,
                  oa_ref, h_ref, xc_ref, dt_ref,
                  exta_ref, extb_ref, conv_ref, s_ref, *, tl):
    first = pl.program_id(1) == 0
    _carry_history(exta_ref, HALO_A, tl, first)
    _carry_history(extb_ref, HALO_B, tl, first)

    x = x_ref[...]
    ms = jnp.mean(x * x, axis=-1, keepdims=True)
    y = x * lax.rsqrt(ms + EPS) * n1g_ref[...]
    h = (y * (1.0 + mod_ref[0, 1:2, :]) + mod_ref[0, 0:1, :]).astype(BF16)
    h_ref[...] = h

    tn = MXU_TILE
    mxu, valu = [], []

    def proj(c0, n=tn):
        return _dot(h, w_ref[:, c0:c0 + n])

    def glu_task(p):
        def fn():
            a_val = proj(C_AVAL + p * tn)
            a_gate = proj(C_AGATE + p * tn)
            _fill_slabs(exta_ref, HALO_A, tl, a_val * _sigmoid(a_gate), p * tn // LANES)
        return fn

    def xbc_task(t):
        def fn():
            _fill_slabs(extb_ref, HALO_B, tl, proj(C_XBC + t * tn), t * tn // LANES)
        return fn

    def outa_task(t):
        def fn():
            out_a = _dot(s_ref[...], wa_ref[:, t * tn:(t + 1) * tn]) + ba_ref[:, t * tn:(t + 1) * tn]
            oa_ref[:, t * tn:(t + 1) * tn] = oa_ref[:, t * tn:(t + 1) * tn] * out_a
        return fn

    def dt_task():
        dt_raw = proj(C_DT, DT_PAD) + dtb_ref[...]
        dt_ref[...] = jnp.maximum(dt_raw, 0.0) + jnp.log1p(jnp.exp(-jnp.abs(dt_raw)))

    def gate_task(t):
        def fn():
            oa_ref[:, t * tn:(t + 1) * tn] = _sigmoid(
                _dot(h, wg_ref[:, t * tn:(t + 1) * tn]) + bg_ref[:, t * tn:(t + 1) * tn])
        return fn

    def conv_a_task(sl):
        def fn():
            for r0 in range(0, tl, CONV_ROWS):
                conv_ref[r0:r0 + CONV_ROWS, sl * LANES:(sl + 1) * LANES] = _causal_conv_slab(
                    exta_ref, sl, cwa_ref, cba_ref, KERNEL_A, HALO_A, r0, CONV_ROWS)
        return fn

    def ln_task(r0):
        def fn():
            v = conv_ref[r0:r0 + CONV_ROWS, :]
            mu = jnp.mean(v, axis=-1, keepdims=True)
            vc = v - mu
            var = jnp.mean(vc * vc, axis=-1, keepdims=True)
            s_ref[r0:r0 + CONV_ROWS, :] = _silu(
                vc * lax.rsqrt(var + EPS) * lg_ref[...] + lb_ref[...]).astype(BF16)
        return fn

    def conv_b_task(sl):
        def fn():
            for r0 in range(0, tl, CONV_ROWS):
                xc_ref[r0:r0 + CONV_ROWS, sl * LANES:(sl + 1) * LANES] = _silu(_causal_conv_slab(
                    extb_ref, sl, cws_ref, cbs_ref, KERNEL_SSM, HALO_B, r0, CONV_ROWS)).astype(BF16)
        return fn

    n_glu = D_CONV_A // tn
    n_xbc = D_XBC // tn
    for p in range(n_glu):
        mxu.append((glu_task(p), 2 * MXU_TILE_COST, 80, 0))
        for sl in range(p * tn // LANES, (p + 1) * tn // LANES):
            valu.append((conv_a_task(sl), 500, len(mxu)))
    for r0 in range(0, tl, CONV_ROWS):
        valu.append((ln_task(r0), 200, len(mxu)))
    n_ln_done = len(valu)
    for t in range(n_xbc):
        mxu.append((xbc_task(t), MXU_TILE_COST, 0, 0))
        for sl in range(t * tn // LANES, (t + 1) * tn // LANES):
            valu.append((conv_b_task(sl), 115, len(mxu)))
    mxu.append((dt_task, MXU_TILE_COST // 2, 40, 0))
    for t in range(D_MODEL // tn):
        mxu.append((gate_task(t), MXU_TILE_COST, 80, 0))
    for t in range(D_MODEL // tn):
        mxu.append((outa_task(t), MXU_TILE_COST, 40, n_ln_done))
    _emit_interleaved(mxu, valu)


def _ssd_kernel(xc_ref, dt_ref, h_ref, oa_ref, x_ref, mod_ref,
                alog_ref, dsk_ref, ng_ref, wb_ref, wo_ref, wz_ref, wgb_ref, bgb_ref,
                o_ref, y_ref, s_ref, yn_ref, mg_ref, zs_ref, gb_ref, *, tl):
    @pl.when(pl.program_id(1) == 0)
    def _():
        s_ref[...] = jnp.zeros(s_ref.shape, s_ref.dtype)

    q = SSD_CHUNK
    a_neg = -jnp.exp(alog_ref[...])
    rows = lax.broadcasted_iota(jnp.int32, (q, q), 0)
    cols = lax.broadcasted_iota(jnp.int32, (q, q), 1)
    causal = rows >= cols
    tril_b = causal.astype(BF16)
    lane = lax.broadcasted_iota(jnp.int32, (q, LANES), 1)
    left = lane < HEAD_DIM

    def chunk_prep(ci):
        r0 = ci * q
        dtq = dt_ref[r0:r0 + q, :]
        da = dtq * a_neg
        da_hi = da.astype(BF16)
        rem = da - da_hi.astype(F32)
        da_mid = rem.astype(BF16)
        da_lo = (rem - da_mid.astype(F32)).astype(BF16)
        csum = _dot(tril_b, jnp.concatenate([da_hi, da_mid, da_lo], axis=1))
        acum = (csum[:, 0:DT_PAD] + csum[:, DT_PAD:2 * DT_PAD]) + csum[:, 2 * DT_PAD:3 * DT_PAD]
        col2 = acum * LOG2_E
        row2 = col2 - jnp.log2(dtq)
        return dict(col2=col2, row2=row2, row2_t=row2.T)

    def group_head(ci, g, ck):
        r0 = ci * q
        c0 = g * GROUP_W
        bg_t = xc_ref[r0:r0 + q, D_INNER + g * D_STATE:D_INNER + (g + 1) * D_STATE].T
        cg = xc_ref[r0:r0 + q, D_INNER + D_BC + g * D_STATE:D_INNER + D_BC + (g + 1) * D_STATE]
        cb_mat = _dot(cg, bg_t)
        y_off = _dot(cg, s_ref[:, c0:c0 + GROUP_W].astype(BF16))
        return cb_mat, y_off, bg_t

    def group_tail(ci, g, ck, cb_mat, y_off, bg_t):
        r0 = ci * q
        c0 = g * GROUP_W
        col2, row2, row2_t = ck["col2"], ck["row2"], ck["row2_t"]
        xw_parts, sdec_parts = [], []
        for pr in range(HEADS_PER_GROUP // 2):
            h1 = g * HEADS_PER_GROUP + 2 * pr
            lanes = slice(c0 + pr * LANES, c0 + (pr + 1) * LANES)

            def scores(h):
                seg2 = col2[:, h:h + 1] - row2_t[h:h + 1, :]
                return (jnp.where(causal, jnp.exp2(seg2), 0.0) * cb_mat).astype(BF16)

            col_p = jnp.where(left, col2[:, h1:h1 + 1], col2[:, h1 + 1:h1 + 2])
            row_p = jnp.where(left, row2[:, h1:h1 + 1], row2[:, h1 + 1:h1 + 2])
            last_p = col_p[q - 1:q, :]
            xp = xc_ref[r0:r0 + q, lanes]
            xw_parts.append((xp.astype(F32) * jnp.exp2(last_p - row_p)).astype(BF16))
            sdec_parts.append(jnp.exp2(last_p))

            lhs = jnp.concatenate([scores(h1), scores(h1 + 1)], axis=1)
            zero = jnp.zeros_like(xp)
            rhs = jnp.concatenate([jnp.where(left, xp, zero),
                                   jnp.where(left, zero, xp)], axis=0)
            y_ref[r0:r0 + q, lanes] = (
                _dot(lhs, rhs) + y_off[:, pr * LANES:(pr + 1) * LANES] * jnp.exp2(col_p))

        new_state = _dot(bg_t, jnp.concatenate(xw_parts, axis=1))
        s_ref[:, c0:c0 + GROUP_W] = (
            s_ref[:, c0:c0 + GROUP_W] * jnp.concatenate(sdec_parts, axis=1) + new_state)

    tn = MXU_TILE
    side = []
    for t in range(D_INNER // tn):
        def z_tile(t=t):
            zs_ref[:, t * tn:(t + 1) * tn] = _silu(_dot(h_ref[...], wz_ref[:, t * tn:(t + 1) * tn])).astype(BF16)
        side.append(z_tile)
    for t in range(D_MODEL // tn):
        def gb_tile(t=t):
            gb_ref[:, t * tn:(t + 1) * tn] = _sigmoid(
                _dot(h_ref[...], wgb_ref[:, t * tn:(t + 1) * tn]) + bgb_ref[:, t * tn:(t + 1) * tn]).astype(BF16)
        side.append(gb_tile)

    n_chunks = tl // q
    ck = chunk_prep(0)
    head = group_head(0, 0, ck)
    for ci in range(n_chunks):
        ck_next = None
        for g in range(N_GROUPS):
            cur = head
            if g + 1 < N_GROUPS:
                head = group_head(ci, g + 1, ck)
            elif ci + 1 < n_chunks:
                ck_next = chunk_prep(ci + 1)
            if side:
                side.pop(0)()
            group_tail(ci, g, ck, *cur)
        if ci + 1 < n_chunks:
            ck = ck_next
            head = group_head(ci + 1, 0, ck)

    while side:
        side.pop(0)()
    y = y_ref[...] + xc_ref[:, 0:D_INNER].astype(F32) * dsk_ref[...]
    y = y * zs_ref[...].astype(F32)
    for g in range(N_GROUPS):
        yg = y[:, g * GROUP_W:(g + 1) * GROUP_W]
        ms = jnp.mean(yg * yg, axis=-1, keepdims=True)
        yn_ref[:, g * GROUP_W:(g + 1) * GROUP_W] = (
            yg * lax.rsqrt(ms + EPS_SSM_NORM) * ng_ref[:, g * GROUP_W:(g + 1) * GROUP_W]).astype(BF16)
    out_b = _dot(yn_ref[...], wb_ref[...])
    mg_ref[...] = (oa_ref[...] + gb_ref[...].astype(F32) * out_b).astype(BF16)
    mix = _dot(mg_ref[...], wo_ref[...])
    o_ref[...] = x_ref[...] + mod_ref[0, 2:3, :] * mix


HALO_F = SUBLANES
COLS_F = 256
FFN_TILE = 512
FFN_HALVES = 2


def _ffn_kernel(x_ref, mod_ref, n2g_ref, wup_ref, cw_ref, cb_ref, wdn_ref, nfg_ref,
                o_ref, ext_ref, h2_ref, *, tl):
    _carry_history(ext_ref, HALO_F, tl, pl.program_id(1) == 0)
    hr = tl // FFN_HALVES

    def rows(hf):
        return slice(hf * hr, (hf + 1) * hr)

    def prologue(hf):
        x = x_ref[rows(hf), :]
        ms = jnp.mean(x * x, axis=-1, keepdims=True)
        y = x * lax.rsqrt(ms + EPS) * n2g_ref[...]
        h2_ref[rows(hf), :] = (y * (1.0 + mod_ref[0, 4:5, :]) + mod_ref[0, 3:4, :]).astype(BF16)

    def up_cols(hf, blk):
        for c0 in (blk * COLS_F, D_FF + blk * COLS_F):
            _fill_slabs(ext_ref, HALO_F, tl, _dot(h2_ref[rows(hf), :], wup_ref[:, c0:c0 + COLS_F]),
                        c0 // LANES, hf * hr)

    def act_cols(hf, blk):
        def conv(c0):
            return jnp.concatenate(
                [_causal_conv_slab(ext_ref, c0 // LANES + i, cw_ref, cb_ref, KERNEL_FFN, HALO_F, hf * hr, hr)
                 for i in range(COLS_F // LANES)], axis=1)
        return (_silu(conv(blk * COLS_F)) * conv(D_FF + blk * COLS_F)).astype(BF16)

    def epilogue(hf, down):
        xo = x_ref[rows(hf), :] + mod_ref[0, 5:6, :] * down
        ms2 = jnp.mean(xo * xo, axis=-1, keepdims=True)
        o_ref[rows(hf), :] = xo * lax.rsqrt(ms2 + EPS) * nfg_ref[...]

    nblk = D_FF // COLS_F
    units = [(hf, blk) for hf in range(FFN_HALVES) for blk in range(nblk)]
    down = [jnp.zeros((hr, D_MODEL), F32) for _ in range(FFN_HALVES)]
    prologue(0)
    up_cols(*units[0])
    for i, (hf, blk) in enumerate(units):
        if hf + 1 < FFN_HALVES and blk == nblk // 2:
            prologue(hf + 1)
        if i + 1 < len(units):
            up_cols(*units[i + 1])
        down[hf] = down[hf] + _dot(act_cols(hf, blk), wdn_ref[blk * COLS_F:(blk + 1) * COLS_F, :])
        if hf > 0 and blk == 1:
            epilogue(hf - 1, down[hf - 1])
    epilogue(FFN_HALVES - 1, down[FFN_HALVES - 1])


def kernel(x, c, w_ada, b_ada, norm1_g, w_in, conv_a_w, conv_a_b, ln_a_g, ln_a_b, w_a_out, b_a_out, conv_ssm_w, conv_ssm_b, dt_bias, a_log, d_skip, ssm_norm_g, w_b_out, w_gate, b_gate, w_o, norm2_g, w_up, conv_ffn_w, conv_ffn_b, w_down, norm_f_g):
    bsz, seqlen, d = x.shape
    assert d == D_MODEL and w_ada.shape[0] == 1
    tl = SEQ_TILE
    assert seqlen % SEQ_TILE == 0 and seqlen % SSD_TILE == 0 and seqlen % FFN_TILE == 0
    assert SSD_TILE % SSD_CHUNK == 0 and FFN_TILE % FFN_HALVES == 0

    row = lambda v: v.reshape(1, -1)
    n_dt = N_HEADS
    w_in_p = jnp.pad(w_in[0], ((0, 0), (0, DT_PAD - n_dt))).astype(BF16)
    dt_bias_pad = jnp.pad(dt_bias[0], (0, DT_PAD - n_dt)).reshape(1, DT_PAD)
    a_log_pad = jnp.pad(a_log[0], (0, DT_PAD - n_dt)).reshape(1, DT_PAD)
    d_skip_x = jnp.repeat(d_skip[0], HEAD_DIM).reshape(1, D_INNER)

    x2d = x.reshape(bsz * seqlen, D_MODEL)
    mod3 = _ada(c, w_ada[0], b_ada[0]).reshape(bsz, N_MOD, D_MODEL)

    oa, hb, xc, dt = _seq_call(
        _front_kernel, "front", bsz, seqlen,
        [(x2d, D_MODEL, 0), (mod3, None, None)],
        [row(norm1_g[0]), w_in_p, w_gate[0][:, :D_MODEL].astype(BF16), row(b_gate[0][:D_MODEL]), dt_bias_pad,
         conv_a_w[0], row(conv_a_b[0]), row(ln_a_g[0]), row(ln_a_b[0]),
         w_a_out[0].astype(BF16), row(b_a_out[0]), conv_ssm_w[0], row(conv_ssm_b[0])],
        [(D_MODEL, F32), (D_MODEL, BF16), (D_XBC, BF16), (DT_PAD, F32)],
        [pltpu.VMEM((D_CONV_A // LANES, HALO_A + tl, LANES), F32),
         pltpu.VMEM((D_XBC // LANES, HALO_B + tl, LANES), F32),
         pltpu.VMEM((tl, D_CONV_A), F32),
         pltpu.VMEM((tl, D_CONV_A), BF16)])

    (x1,) = _seq_call(
        _ssd_kernel, "ssd", bsz, seqlen,
        [(xc, D_XBC, 0), (dt, DT_PAD, 0), (hb, D_MODEL, 0), (oa, D_MODEL, 0),
         (x2d, D_MODEL, 0), (mod3, None, None)],
        [a_log_pad, d_skip_x, row(ssm_norm_g[0]),
         w_b_out[0].astype(BF16), w_o[0].astype(BF16),
         w_in[0][:, C_Z:C_XBC].astype(BF16), w_gate[0][:, D_MODEL:].astype(BF16), row(b_gate[0][D_MODEL:])],
        [(D_MODEL, F32)],
        [pltpu.VMEM((SSD_TILE, D_INNER), F32), pltpu.VMEM((D_STATE, D_INNER), F32),
         pltpu.VMEM((SSD_TILE, D_INNER), BF16), pltpu.VMEM((SSD_TILE, D_MODEL), BF16),
         pltpu.VMEM((SSD_TILE, D_INNER), BF16), pltpu.VMEM((SSD_TILE, D_MODEL), BF16)], tl=SSD_TILE)

    (out,) = _seq_call(
        _ffn_kernel, "ffn", bsz, seqlen,
        [(x1, D_MODEL, 0), (mod3, None, None)],
        [row(norm2_g[0]), w_up[0].astype(BF16), conv_ffn_w[0], row(conv_ffn_b[0]),
         w_down[0].astype(BF16), row(norm_f_g)],
        [(D_MODEL, F32)],
        [pltpu.VMEM((2 * D_FF // LANES, HALO_F + FFN_TILE, LANES), F32),
         pltpu.VMEM((FFN_TILE, D_MODEL), BF16)], tl=FFN_TILE)
    return out.reshape(bsz, seqlen, D_MODEL)
```
